```python
import math
import jax, jax.numpy as jnp
from jax import lax
import numpy as np

D_MODEL = 4096
BATCH = 1
SEQ = 16384
DEPTH = 2
DEC_BATCH = 8
DEC_SEQ = 64
PAST_LEN = 4096

CHUNK = 64
N_MIXERS = 2
N_SSM_LAYERS = (DEPTH + N_MIXERS - 1) // N_MIXERS
N_ATTN_LAYERS = DEPTH // N_MIXERS
SSM_GROUP = 16
SSM_GROUPS = D_MODEL // SSM_GROUP
SSM_STATE = 64
N_HEADS = 16
HEAD_DIM = D_MODEL // (2 * N_HEADS)
V_HEAD_DIM = 2 * HEAD_DIM
QK_WIDTH = N_HEADS * 2 * HEAD_DIM
V_WIDTH = N_HEADS * V_HEAD_DIM
Q_BLOCK = 128
N_EXPERTS = 32
N_EXPERT_GROUPS = 8
EXPERTS_PER_GROUP = N_EXPERTS // N_EXPERT_GROUPS
TOP_K = 2
D_FF_EXPERT = D_MODEL // 4
MOE_BLOCK = 128
NORM_EPS = 1e-6
NEG_INF = -1e30

kernel_name = 'streaming_s5_diffattn_moe_step'


def _rms_norm(x, g):
    xf = x.astype(jnp.float32)
    y = xf * lax.rsqrt(jnp.mean(xf * xf, axis=-1, keepdims=True) + NORM_EPS)
    return (y * g.astype(jnp.float32)).astype(x.dtype)


def _s5_discretize(a_re, a_im, log_dt, b_re, b_im):
    a_re = a_re.astype(jnp.float32)
    a_im = a_im.astype(jnp.float32)
    dt = jnp.exp(log_dt.astype(jnp.float32))[:, None]
    mag = jnp.exp(a_re * dt)
    abar_re = mag * jnp.cos(a_im * dt)
    abar_im = mag * jnp.sin(a_im * dt)
    den = a_re * a_re + a_im * a_im
    nr = abar_re - 1.0
    coef_re = (nr * a_re + abar_im * a_im) / den
    coef_im = (abar_im * a_re - nr * a_im) / den
    b_re = b_re.astype(jnp.float32)
    b_im = b_im.astype(jnp.float32)
    bbar_re = coef_re[..., None] * b_re - coef_im[..., None] * b_im
    bbar_im = coef_re[..., None] * b_im + coef_im[..., None] * b_re
    return abar_re, abar_im, bbar_re, bbar_im


def _s5_combine(earlier, later):
    ar1, ai1, br1, bi1 = earlier
    ar2, ai2, br2, bi2 = later
    return (ar2 * ar1 - ai2 * ai1,
            ar2 * ai1 + ai2 * ar1,
            ar2 * br1 - ai2 * bi1 + br2,
            ar2 * bi1 + ai2 * br1 + bi2)


def _s5_mixer(h, h0_re, h0_im, a_re, a_im, log_dt, b_re, b_im, c_re, c_im, d, glu_w, glu_b):
    bsz, t_len, _ = h.shape
    abar_re, abar_im, bbar_re, bbar_im = _s5_discretize(a_re, a_im, log_dt, b_re, b_im)
    c_re = c_re.astype(jnp.float32)
    c_im = c_im.astype(jnp.float32)
    u = h.astype(jnp.float32)
    blk = CHUNK if t_len % CHUNK == 0 else t_len
    n_blk = t_len // blk
    u_blocks = u.reshape(bsz, n_blk, blk, SSM_GROUPS, SSM_GROUP).swapaxes(0, 1)

    def step(carry, u_blk):
        s_re, s_im = carry
        bu_re = jnp.einsum('blgh,gph->blgp', u_blk, bbar_re)
        bu_im = jnp.einsum('blgh,gph->blgp', u_blk, bbar_im)
        a_re_b = jnp.broadcast_to(abar_re, bu_re.shape)
        a_im_b = jnp.broadcast_to(abar_im, bu_im.shape)
        cum_re, cum_im, acc_re, acc_im = lax.associative_scan(
            _s5_combine, (a_re_b, a_im_b, bu_re, bu_im), axis=1)
        hs_re = acc_re + cum_re * s_re[:, None] - cum_im * s_im[:, None]
        hs_im = acc_im + cum_re * s_im[:, None] + cum_im * s_re[:, None]
        y = (jnp.einsum('blgp,ghp->blgh', hs_re, c_re)
             - jnp.einsum('blgp,ghp->blgh', hs_im, c_im))
        return (hs_re[:, -1], hs_im[:, -1]), y

    (s_re, s_im), ys = lax.scan(step, (h0_re.astype(jnp.float32), h0_im.astype(jnp.float32)), u_blocks)
    y = ys.swapaxes(0, 1).reshape(bsz, t_len, D_MODEL) + d.astype(jnp.float32) * u
    g = jax.nn.gelu(y)
    out = g * jax.nn.sigmoid(g @ glu_w.astype(jnp.float32) + glu_b.astype(jnp.float32))
    return out.astype(h.dtype), s_re, s_im


def _diff_attn_block(qb, qp, k, v, kp, lam):
    slopes = 2.0 ** (-8.0 * jnp.arange(1, N_HEADS + 1, dtype=jnp.float32) / N_HEADS)
    s = jnp.einsum('bqhmd,bkhmd->bhmqk', qb, k).astype(jnp.float32) * (HEAD_DIM ** -0.5)
    dist = jnp.abs(qp[:, None] - kp[None, :]).astype(jnp.float32)
    visible = (kp[None, :] // CHUNK) <= (qp[:, None] // CHUNK)
    s = s - slopes[None, :, None, None, None] * dist
    s = jnp.where(visible, s, NEG_INF)
    p = jax.nn.softmax(s, axis=-1)
    w = p[:, :, 0] - lam * p[:, :, 1]
    return jnp.einsum('bhqk,bkhe->bqhe', w.astype(v.dtype), v)


def _diff_attention(q, k, v, q_pos, k_pos, lam):
    bsz, t_len = q.shape[:2]
    if t_len > Q_BLOCK:
        n_b = t_len // Q_BLOCK
        qb = q.reshape(bsz, n_b, Q_BLOCK, N_HEADS, 2, HEAD_DIM).swapaxes(0, 1)
        qp = q_pos.reshape(n_b, Q_BLOCK)
        o = lax.map(lambda a: _diff_attn_block(a[0], a[1], k, v, k_pos, lam), (qb, qp))
        return o.swapaxes(0, 1).reshape(bsz, t_len, N_HEADS, V_HEAD_DIM)
    return _diff_attn_block(q, q_pos, k, v, k_pos, lam)


def _diff_attn_mixer(h, past_k, past_v, w_qkv, w_o, q_norm, k_norm, lam_p, subln, lam_init):
    bsz, t_len, _ = h.shape
    qkv = h @ w_qkv
    q = qkv[..., :QK_WIDTH].reshape(bsz, t_len, N_HEADS, 2, HEAD_DIM)
    k = qkv[..., QK_WIDTH:2 * QK_WIDTH].reshape(bsz, t_len, N_HEADS, 2, HEAD_DIM)
    v = qkv[..., 2 * QK_WIDTH:].reshape(bsz, t_len, N_HEADS, V_HEAD_DIM)
    q = _rms_norm(q, q_norm)
    k = _rms_norm(k, k_norm)
    past_len = 0 if past_k is None else past_k.shape[1]
    k_all = k if past_k is None else jnp.concatenate([past_k.astype(k.dtype), k], axis=1)
    v_all = v if past_v is None else jnp.concatenate([past_v.astype(v.dtype), v], axis=1)
    q_pos = past_len + jnp.arange(t_len, dtype=jnp.int32)
    k_pos = jnp.arange(past_len + t_len, dtype=jnp.int32)
    lp = lam_p.astype(jnp.float32)
    lam = jnp.exp(jnp.sum(lp[0] * lp[1])) - jnp.exp(jnp.sum(lp[2] * lp[3])) + lam_init
    o = _diff_attention(q, k_all, v_all, q_pos, k_pos, lam)
    o = _rms_norm(o, subln) * (1.0 - lam_init)
    return o.reshape(bsz, t_len, V_WIDTH).astype(h.dtype) @ w_o, k, v


def _route(xt, router_w, router_b):
    scores = jax.nn.sigmoid((xt @ router_w).astype(jnp.float32))
    sel = scores + router_b.astype(jnp.float32)
    sel_g = sel.reshape(-1, N_EXPERT_GROUPS, EXPERTS_PER_GROUP)
    group_score = lax.top_k(sel_g, 2)[0].sum(-1)
    g_idx = jnp.argmax(group_score, axis=-1).astype(jnp.int32)
    in_group = jnp.take_along_axis(sel_g, g_idx[:, None, None], axis=1)[:, 0]
    _, local = lax.top_k(in_group, TOP_K)
    expert_idx = (g_idx[:, None] * EXPERTS_PER_GROUP + local).astype(jnp.int32)
    w = jnp.take_along_axis(scores, expert_idx, axis=1)
    return expert_idx, w / jnp.sum(w, axis=-1, keepdims=True)


def _moe(h, router_w, router_b, w1, w3, w2):
    bsz, t_len, _ = h.shape
    xt = h.reshape(bsz * t_len, D_MODEL)
    n_tok = bsz * t_len
    expert_idx, gate = _route(xt, router_w, router_b)
    n_assign = n_tok * TOP_K
    flat_e = expert_idx.reshape(n_assign)
    order = jnp.argsort(flat_e, stable=True).astype(jnp.int32)
    sorted_e = flat_e[order]
    counts = jnp.zeros((N_EXPERTS,), jnp.int32).at[flat_e].add(1)
    padded = (counts + MOE_BLOCK - 1) // MOE_BLOCK * MOE_BLOCK
    pad_end = jnp.cumsum(padded)
    pad_start = pad_end - padded
    start = jnp.cumsum(counts) - counts
    dest = pad_start[sorted_e] + jnp.arange(n_assign, dtype=jnp.int32) - start[sorted_e]
    n_blocks = -(-n_assign // MOE_BLOCK) + N_EXPERTS
    slot_src = jnp.full((n_blocks * MOE_BLOCK,), n_assign, jnp.int32).at[dest].set(order)
    slot_tok = slot_src // TOP_K
    slot_gate = jnp.concatenate([gate.reshape(n_assign), jnp.zeros((1,), gate.dtype)])[slot_src]
    xs = jnp.concatenate([xt, jnp.zeros((1, D_MODEL), xt.dtype)])[slot_tok]
    xs = xs.reshape(n_blocks, MOE_BLOCK, D_MODEL)
    block_e = jnp.minimum(
        jnp.searchsorted(pad_end, jnp.arange(n_blocks, dtype=jnp.int32) * MOE_BLOCK, side='right'),
        N_EXPERTS - 1)

    def expert_block(args):
        xb, e = args
        hid = jax.nn.silu(xb @ w1[e]) * (xb @ w3[e])
        return hid @ w2[e]

    ys = lax.map(expert_block, (xs, block_e)).reshape(n_blocks * MOE_BLOCK, D_MODEL)
    y = jnp.zeros((n_tok + 1, D_MODEL), ys.dtype).at[slot_tok].add(
        ys * slot_gate[:, None].astype(ys.dtype))
    return y[:n_tok].reshape(bsz, t_len, D_MODEL)


def setup_inputs(seed: int = 0) -> dict:
    key = jax.random.key(seed)
    k = jax.random.split(key, 40)
    D = D_MODEL
    f32 = jnp.float32

    def nrm(kk, shape, std=1.0):
        return std * jax.random.normal(kk, shape, f32)

    return {
        'x_prompt': nrm(k[0], (BATCH, SEQ, D)),
        'x_sample': nrm(k[1], (DEC_BATCH, DEC_SEQ, D)),
        'c_prompt': nrm(k[2], (BATCH, D)),
        'c_sample': nrm(k[3], (DEC_BATCH, D)),
        'state_ssm_re': nrm(k[4], (N_SSM_LAYERS, DEC_BATCH, SSM_GROUPS, SSM_STATE), 0.1),
        'state_ssm_im': nrm(k[5], (N_SSM_LAYERS, DEC_BATCH, SSM_GROUPS, SSM_STATE), 0.1),
        'cache_k': nrm(k[6], (N_ATTN_LAYERS, DEC_BATCH, PAST_LEN, N_HEADS, 2, HEAD_DIM)),
        'cache_v': nrm(k[7], (N_ATTN_LAYERS, DEC_BATCH, PAST_LEN, N_HEADS, V_HEAD_DIM)),
        'ada_w': nrm(k[8], (DEPTH, D, 6 * D), 0.5 * D ** -0.5),
        'ada_b': nrm(k[9], (DEPTH, 6 * D), 0.02),
        'norm_mix': 1.0 + nrm(k[10], (DEPTH, D), 0.02),
        'norm_ffn': 1.0 + nrm(k[11], (DEPTH, D), 0.02),
        'ssm_a_re': -0.5 + nrm(k[12], (N_SSM_LAYERS, SSM_GROUPS, SSM_STATE), 0.01),
        'ssm_a_im': math.pi * jnp.arange(SSM_STATE, dtype=f32) + nrm(k[13], (N_SSM_LAYERS, SSM_GROUPS, SSM_STATE), 0.01),
        'ssm_log_dt': jax.random.uniform(k[14], (N_SSM_LAYERS, SSM_GROUPS), f32, math.log(1e-3), math.log(1e-1)),
        'ssm_b_re': nrm(k[15], (N_SSM_LAYERS, SSM_GROUPS, SSM_STATE, SSM_GROUP), (2 * SSM_GROUP) ** -0.5),
        'ssm_b_im': nrm(k[16], (N_SSM_LAYERS, SSM_GROUPS, SSM_STATE, SSM_GROUP), (2 * SSM_GROUP) ** -0.5),
        'ssm_c_re': nrm(k[17], (N_SSM_LAYERS, SSM_GROUPS, SSM_GROUP, SSM_STATE), SSM_STATE ** -0.5),
        'ssm_c_im': nrm(k[18], (N_SSM_LAYERS, SSM_GROUPS, SSM_GROUP, SSM_STATE), SSM_STATE ** -0.5),
        'ssm_d': nrm(k[19], (N_SSM_LAYERS, D)),
        'ssm_glu_w': nrm(k[20], (N_SSM_LAYERS, D, D), D ** -0.5),
        'ssm_glu_b': nrm(k[21], (N_SSM_LAYERS, D), 0.02),
        'attn_w_qkv': nrm(k[22], (N_ATTN_LAYERS, D, 2 * QK_WIDTH + V_WIDTH), D ** -0.5),
        'attn_w_o': nrm(k[23], (N_ATTN_LAYERS, V_WIDTH, D), V_WIDTH ** -0.5),
        'attn_q_norm': 1.0 + nrm(k[24], (N_ATTN_LAYERS, HEAD_DIM), 0.02),
        'attn_k_norm': 1.0 + nrm(k[25], (N_ATTN_LAYERS, HEAD_DIM), 0.02),
        'attn_lambda': nrm(k[26], (N_ATTN_LAYERS, 4, HEAD_DIM), 0.1),
        'attn_subln': 1.0 + nrm(k[27], (N_ATTN_LAYERS, V_HEAD_DIM), 0.02),
        'router_w': nrm(k[28], (D, N_EXPERTS), D ** -0.5),
        'router_b': nrm(k[29], (N_EXPERTS,), 0.01),
        'moe_w1': nrm(k[30], (DEPTH, N_EXPERTS, D, D_FF_EXPERT), D ** -0.5),
        'moe_w3': nrm(k[31], (DEPTH, N_EXPERTS, D, D_FF_EXPERT), D ** -0.5),
        'moe_w2': nrm(k[32], (DEPTH, N_EXPERTS, D_FF_EXPERT, D), D_FF_EXPERT ** -0.5),
    }


def reference(x_prompt, x_sample, c_prompt, c_sample, state_ssm_re, state_ssm_im, cache_k, cache_v,
              ada_w, ada_b, norm_mix, norm_ffn,
              ssm_a_re, ssm_a_im, ssm_log_dt, ssm_b_re, ssm_b_im, ssm_c_re, ssm_c_im, ssm_d,
              ssm_glu_w, ssm_glu_b,
              attn_w_qkv, attn_w_o, attn_q_norm, attn_k_norm, attn_lambda, attn_subln,
              router_w, router_b, moe_w1, moe_w3, moe_w2):

    def trunk(x, c, h0_re, h0_im, past_k, past_v):
        new_re, new_im, new_k, new_v = [], [], [], []
        for i in range(DEPTH):
            mod = (jax.nn.silu(c) @ ada_w[i] + ada_b[i]).reshape(c.shape[0], 6, D_MODEL)[:, :, None, :]
            shift1, scale1, gate1, shift2, scale2, gate2 = (mod[:, m] for m in range(6))
            hn = _rms_norm(x, norm_mix[i]) * (1.0 + scale1) + shift1
            j = i // N_MIXERS
            if i % N_MIXERS == 0:
                out, s_re, s_im = _s5_mixer(hn, h0_re[j], h0_im[j], ssm_a_re[j], ssm_a_im[j], ssm_log_dt[j],
                                            ssm_b_re[j], ssm_b_im[j], ssm_c_re[j], ssm_c_im[j], ssm_d[j],
                                            ssm_glu_w[j], ssm_glu_b[j])
                new_re.append(s_re)
                new_im.append(s_im)
            else:
                lam_init = 0.8 - 0.6 * math.exp(-0.3 * i)
                out, k_rows, v_rows = _diff_attn_mixer(
                    hn, None if past_k is None else past_k[j], None if past_v is None else past_v[j],
                    attn_w_qkv[j], attn_w_o[j], attn_q_norm[j], attn_k_norm[j], attn_lambda[j],
                    attn_subln[j], lam_init)
                new_k.append(k_rows)
                new_v.append(v_rows)
            x = x + gate1 * out
            hn = _rms_norm(x, norm_ffn[i]) * (1.0 + scale2) + shift2
            x = x + gate2 * _moe(hn, router_w, router_b, moe_w1[i], moe_w3[i], moe_w2[i])
        return x, jnp.stack(new_re), jnp.stack(new_im), jnp.stack(new_k), jnp.stack(new_v)

    zero_state = jnp.zeros((N_SSM_LAYERS, x_prompt.shape[0], SSM_GROUPS, SSM_STATE), jnp.float32)
    y_prompt, re_p, im_p, k_p, v_p = trunk(x_prompt, c_prompt, zero_state, zero_state, None, None)
    y_sample, re_s, im_s, k_s, v_s = trunk(x_sample, c_sample, state_ssm_re, state_ssm_im, cache_k, cache_v)
    return (y_prompt, y_sample, re_p, im_p, k_p, v_p, re_s, im_s, k_s, v_s)
```

```python
import functools
import math

import numpy as np
import jax
import jax.numpy as jnp
from jax import lax
from jax.experimental import pallas as pl
from jax.experimental.pallas import tpu as pltpu

F32 = jnp.float32
BF16 = jnp.bfloat16
HIGHEST = lax.Precision.HIGHEST

NORM_EPS = 1e-6
CHUNK = 64
SSM_GROUP = 16
SSM_STATE = 64
S5_L = 16
HEAD_DIM = 128
N_EXPERT_GROUPS = 8
TOP_K = 2
NEG_INF = -1e30
SUBLANES = 8
VMEM_LIMIT = 52 * 1024 * 1024


def _cparams(sem):
    return pltpu.CompilerParams(dimension_semantics=sem, vmem_limit_bytes=VMEM_LIMIT)


def _tile(n, pref):
    if n <= pref:
        return n
    t = pref
    while n % t:
        t //= 2
    return t


def _ada_kernel(c_ref, w_ref, b_ref, o_ref):
    c = c_ref[...]
    s = c * jax.nn.sigmoid(c)
    o_ref[0] = jnp.dot(s, w_ref[0], preferred_element_type=F32, precision=HIGHEST) + b_ref[0]


def _ada_mod(c_all, ada_w, ada_b):
    depth, d, n6 = ada_w.shape
    r = c_all.shape[0]
    tn = _tile(n6, 512)
    return pl.pallas_call(
        _ada_kernel,
        grid=(depth, n6 // tn),
        in_specs=[pl.BlockSpec((r, d), lambda i, j: (0, 0)),
                  pl.BlockSpec((1, d, tn), lambda i, j: (i, 0, j)),
                  pl.BlockSpec((1, 1, tn), lambda i, j: (i, 0, j))],
        out_specs=pl.BlockSpec((1, r, tn), lambda i, j: (i, 0, j)),
        out_shape=jax.ShapeDtypeStruct((depth, r, n6), F32),
        compiler_params=_cparams(("arbitrary", "arbitrary")),
        name="ada_mod",
    )(c_all, ada_w, ada_b.reshape(depth, 1, n6))


def _normmod_kernel(x_ref, g_ref, sc_ref, sh_ref, o_ref):
    x = x_ref[0]
    y = x * lax.rsqrt(jnp.mean(x * x, axis=-1, keepdims=True) + NORM_EPS)
    o_ref[0] = (y * g_ref[...] * (1.0 + sc_ref[0]) + sh_ref[0]).astype(o_ref.dtype)


def _normmod_router_kernel(x_ref, g_ref, sc_ref, sh_ref, rw_ref, o_ref, lg_ref):
    x = x_ref[0]
    y = x * lax.rsqrt(jnp.mean(x * x, axis=-1, keepdims=True) + NORM_EPS)
    hn = y * g_ref[...] * (1.0 + sc_ref[0]) + sh_ref[0]
    o_ref[0] = hn.astype(o_ref.dtype)
    lg_ref[0] = jnp.dot(hn, rw_ref[...], preferred_element_type=F32, precision=HIGHEST)


def _norm_mod(x, g, scale, shift, router_w=None):
    b, t, d = x.shape
    tm = _tile(t, 256)
    xspec = pl.BlockSpec((1, tm, d), lambda i, j: (i, j, 0))
    vspec = pl.BlockSpec((1, 1, d), lambda i, j: (i, 0, 0))
    gspec = pl.BlockSpec((1, d), lambda i, j: (0, 0))
    if router_w is None:
        return pl.pallas_call(
            _normmod_kernel, grid=(b, t // tm),
            in_specs=[xspec, gspec, vspec, vspec], out_specs=xspec,
            out_shape=jax.ShapeDtypeStruct((b, t, d), BF16),
            compiler_params=_cparams(("arbitrary", "arbitrary")), name="norm_mod",
        )(x, g.reshape(1, d), scale, shift)
    e = router_w.shape[1]
    return pl.pallas_call(
        _normmod_router_kernel, grid=(b, t // tm),
        in_specs=[xspec, gspec, vspec, vspec, pl.BlockSpec((d, e), lambda i, j: (0, 0))],
        out_specs=[xspec, pl.BlockSpec((1, tm, e), lambda i, j: (i, j, 0))],
        out_shape=[jax.ShapeDtypeStruct((b, t, d), BF16), jax.ShapeDtypeStruct((b, t, e), F32)],
        compiler_params=_cparams(("arbitrary", "arbitrary")), name="norm_mod_router",
    )(x, g.reshape(1, d), scale, shift, router_w)


def _cmul(ar, ai, br, bi):
    return ar * br - ai * bi, ar * bi + ai * br


def _cpow(re, im, n):
    out_re, out_im = jnp.ones_like(re), jnp.zeros_like(im)
    while n:
        if n & 1:
            out_re, out_im = _cmul(out_re, out_im, re, im)
        re, im = _cmul(re, im, re, im)
        n >>= 1
    return out_re, out_im


def _s5_params(a_re, a_im, log_dt, b_re, b_im, c_re, c_im, d, seg_len):
    g, p = a_re.shape
    h = SSM_GROUP
    L = S5_L
    q = g // 2
    dt = jnp.exp(log_dt.astype(F32))[:, None]
    mag = jnp.exp(a_re * dt)
    abar_re = mag * jnp.cos(a_im * dt)
    abar_im = mag * jnp.sin(a_im * dt)
    den = a_re * a_re + a_im * a_im
    nr = abar_re - 1.0
    coef_re = (nr * a_re + abar_im * a_im) / den
    coef_im = (abar_im * a_re - nr * a_im) / den
    bbar_re = coef_re[..., None] * b_re - coef_im[..., None] * b_im
    bbar_im = coef_re[..., None] * b_im + coef_im[..., None] * b_re
    pw_re, pw_im = [jnp.ones_like(abar_re)], [jnp.zeros_like(abar_im)]
    for _ in range(L):
        r_, i_ = _cmul(pw_re[-1], pw_im[-1], abar_re, abar_im)
        pw_re.append(r_)
        pw_im.append(i_)
    pw_re = jnp.stack(pw_re)
    pw_im = jnp.stack(pw_im)
    w1_re = pw_re[:L, :, :, None] * bbar_re[None] - pw_im[:L, :, :, None] * bbar_im[None]
    w1_im = pw_re[:L, :, :, None] * bbar_im[None] + pw_im[:L, :, :, None] * bbar_re[None]
    k_tau = (jnp.einsum('gop,tgpi->tgio', c_re, w1_re, precision=HIGHEST)
             - jnp.einsum('gop,tgpi->tgio', c_im, w1_im, precision=HIGHEST))
    lag = np.arange(L)[None, :] - np.arange(L)[:, None]
    m5 = jnp.where((lag >= 0)[:, :, None, None, None], k_tau[np.maximum(lag, 0)], 0.0)
    m_mat = m5.transpose(2, 0, 3, 1, 4).reshape(g, L * h, L * h)
    ws_re = w1_re[::-1].transpose(1, 0, 3, 2).reshape(g, L * h, p)
    ws_im = w1_im[::-1].transpose(1, 0, 3, 2).reshape(g, L * h, p)
    z = jnp.zeros_like(ws_re[0::2])
    ws_top = jnp.concatenate([ws_re[0::2], z, ws_im[0::2], z], axis=-1)
    ws_bot = jnp.concatenate([z, ws_re[1::2], z, ws_im[1::2]], axis=-1)
    ws_pair = jnp.concatenate([ws_top, ws_bot], axis=1)
    x_re = c_re[None] * pw_re[1:, :, None, :] - c_im[None] * pw_im[1:, :, None, :]
    x_im = c_re[None] * pw_im[1:, :, None, :] + c_im[None] * pw_re[1:, :, None, :]
    wy_re = x_re.transpose(1, 3, 0, 2).reshape(g, p, L * h)
    wy_im = (-x_im).transpose(1, 3, 0, 2).reshape(g, p, L * h)
    zy = jnp.zeros_like(wy_re[0::2])
    wy_pair = jnp.concatenate([
        jnp.concatenate([wy_re[0::2], zy], axis=-1),
        jnp.concatenate([zy, wy_re[1::2]], axis=-1),
        jnp.concatenate([wy_im[0::2], zy], axis=-1),
        jnp.concatenate([zy, wy_im[1::2]], axis=-1)], axis=1)
    achunk_re, achunk_im = pw_re[L], pw_im[L]
    aseg_re, aseg_im = _cpow(achunk_re, achunk_im, seg_len)
    d_pair = jnp.broadcast_to(d.reshape(q, 2, 1, h), (q, 2, L, h)).reshape(q, 1, 2 * L * h)
    return dict(m=m_mat.astype(BF16), ws=ws_pair.astype(BF16), wy=wy_pair.astype(BF16),
                ac_re=achunk_re.reshape(q, 1, 2 * p), ac_im=achunk_im.reshape(q, 1, 2 * p),
                as_re=aseg_re.reshape(q, 1, 2 * p), as_im=aseg_im.reshape(q, 1, 2 * p), d=d_pair)


def _s5_local_kernel(u_ref, ws_ref, sre_ref, sim_ref, *, pb):
    for i in range(pb):
        s = jnp.dot(u_ref[i], ws_ref[i], preferred_element_type=F32)
        sre_ref[i] = s[:, :128]
        sim_ref[i] = s[:, 128:]


def _s5_scan_kernel(sre_ref, sim_ref, ac_re_ref, ac_im_ref, as_re_ref, as_im_ref, i_re_ref, i_im_ref,
                    hre_ref, him_ref, fre_ref, fim_ref, *, seg_len, chained, pb):
    shape = (SUBLANES, 128)
    ar = [jnp.broadcast_to(ac_re_ref[j], shape) for j in range(pb)]
    ai = [jnp.broadcast_to(ac_im_ref[j], shape) for j in range(pb)]

    def advance(i, carry):
        out = []
        for j in range(pb):
            hr, hi = carry[2 * j], carry[2 * j + 1]
            out.append(ar[j] * hr - ai[j] * hi + sre_ref[j, pl.ds(i, SUBLANES, stride=seg_len), :])
            out.append(ar[j] * hi + ai[j] * hr + sim_ref[j, pl.ds(i, SUBLANES, stride=seg_len), :])
        return tuple(out)

    def store_advance(i, carry):
        for j in range(pb):
            hre_ref[j, pl.ds(i, SUBLANES, stride=seg_len), :] = carry[2 * j]
            him_ref[j, pl.ds(i, SUBLANES, stride=seg_len), :] = carry[2 * j + 1]
        return advance(i, carry)

    start = []
    for j in range(pb):
        start += [i_re_ref[j], i_im_ref[j]]
    start = tuple(start)
    if chained:
        ends = lax.fori_loop(0, seg_len, advance, start)
        row = lax.broadcasted_iota(jnp.int32, shape, 0)
        fixed = []
        for j in range(pb):
            er, ei = ends[2 * j], ends[2 * j + 1]
            sr, si = as_re_ref[j], as_im_ref[j]
            tr, ti = er[0:1], ei[0:1]
            t_re, t_im = start[2 * j], start[2 * j + 1]
            for s in range(1, SUBLANES):
                if s > 1:
                    tr, ti = sr * tr - si * ti + er[s - 1:s], sr * ti + si * tr + ei[s - 1:s]
                t_re = jnp.where(row == s, jnp.broadcast_to(tr, shape), t_re)
                t_im = jnp.where(row == s, jnp.broadcast_to(ti, shape), t_im)
            fixed += [t_re, t_im]
        start = tuple(fixed)
    final = lax.fori_loop(0, seg_len, store_advance, start)
    for j in range(pb):
        fre_ref[j] = final[2 * j]
        fim_ref[j] = final[2 * j + 1]


def _s5_out_kernel(u_ref, m_ref, wy_ref, hre_ref, him_ref, d_ref, o_ref, *, pb):
    for i in range(pb):
        u = u_ref[i]
        hp = jnp.concatenate([hre_ref[i], him_ref[i]], axis=-1).astype(BF16)
        y = jnp.dot(hp, wy_ref[i], preferred_element_type=F32)
        y0 = jnp.dot(u[:, :256], m_ref[2 * i], preferred_element_type=F32)
        y1 = jnp.dot(u[:, 256:], m_ref[2 * i + 1], preferred_element_type=F32)
        y = y + jnp.concatenate([y0, y1], axis=-1) + d_ref[i] * u.astype(F32)
        o_ref[i] = jax.nn.gelu(y, approximate=True).astype(o_ref.dtype)


def _s5_mixer(hn, h0_re, h0_im, prm, chained):
    b, t, d = hn.shape
    g = d // SSM_GROUP
    q = g // 2
    L = S5_L
    nc = t // L
    r = b * nc
    w = g * SSM_STATE
    seg_len = r // SUBLANES
    lw = 2 * L * SSM_GROUP
    u = hn.reshape(b, nc, L, q, 2, SSM_GROUP).transpose(3, 0, 1, 4, 2, 5).reshape(q, r, lw)
    pb = _tile(q, 4)
    blk = pl.BlockSpec((pb, r, 128), lambda i: (i, 0, 0))
    s_re, s_im = pl.pallas_call(
        functools.partial(_s5_local_kernel, pb=pb), grid=(q // pb,),
        in_specs=[pl.BlockSpec((pb, r, lw), lambda i: (i, 0, 0)),
                  pl.BlockSpec((pb, lw, 256), lambda i: (i, 0, 0))],
        out_specs=[blk, blk],
        out_shape=[jax.ShapeDtypeStruct((q, r, 128), F32)] * 2,
        compiler_params=_cparams(("arbitrary",)), name="s5_local",
    )(u, prm['ws'])
    rowspec = pl.BlockSpec((pb, 1, 128), lambda i: (i, 0, 0))
    st = pl.BlockSpec((pb, SUBLANES, 128), lambda i: (i, 0, 0))
    to_pairs = lambda a: a.reshape(SUBLANES, q, 128).transpose(1, 0, 2)
    h_re, h_im, f_re, f_im = pl.pallas_call(
        functools.partial(_s5_scan_kernel, seg_len=seg_len, chained=chained, pb=pb), grid=(q // pb,),
        in_specs=[blk, blk, rowspec, rowspec, rowspec, rowspec, st, st],
        out_specs=[blk, blk, st, st],
        out_shape=[jax.ShapeDtypeStruct((q, r, 128), F32)] * 2
        + [jax.ShapeDtypeStruct((q, SUBLANES, 128), F32)] * 2,
        compiler_params=_cparams(("arbitrary",)), name="s5_scan",
    )(s_re, s_im, prm['ac_re'], prm['ac_im'], prm['as_re'], prm['as_im'], to_pairs(h0_re), to_pairs(h0_im))
    gl = pl.pallas_call(
        functools.partial(_s5_out_kernel, pb=pb), grid=(q // pb,),
        in_specs=[pl.BlockSpec((pb, r, lw), lambda i: (i, 0, 0)),
                  pl.BlockSpec((2 * pb, 256, 256), lambda i: (i, 0, 0)),
                  pl.BlockSpec((pb, 256, lw), lambda i: (i, 0, 0)),
                  blk, blk,
                  pl.BlockSpec((pb, 1, lw), lambda i: (i, 0, 0))],
        out_specs=pl.BlockSpec((pb, r, lw), lambda i: (i, 0, 0)),
        out_shape=jax.ShapeDtypeStruct((q, r, lw), BF16),
        compiler_params=_cparams(("arbitrary",)), name="s5_out",
    )(u, prm['m'], prm['wy'], h_re, h_im, prm['d'])
    gl = gl.reshape(q, b, nc, 2, L, SSM_GROUP).transpose(1, 2, 4, 0, 3, 5).reshape(b, t, d)
    from_pairs = lambda a: a.transpose(1, 0, 2).reshape(SUBLANES, w)
    return gl, from_pairs(f_re), from_pairs(f_im)


def _glu_kernel(g_ref, w_ref, b_ref, gt_ref, x_ref, gate_ref, o_ref):
    acc = jnp.dot(g_ref[0], w_ref[...], preferred_element_type=F32) + b_ref[...]
    o_ref[0] = x_ref[0] + gate_ref[0] * (gt_ref[0].astype(F32) * jax.nn.sigmoid(acc))


def _proj_res_kernel(a_ref, w_ref, x_ref, gate_ref, o_ref):
    acc = jnp.dot(a_ref[0], w_ref[...], preferred_element_type=F32)
    o_ref[0] = x_ref[0] + gate_ref[0] * acc


def _head_rms(acc, gain):
    outs = []
    for s in range(acc.shape[1] // HEAD_DIM):
        a = acc[:, s * HEAD_DIM:(s + 1) * HEAD_DIM]
        outs.append(a * lax.rsqrt(jnp.mean(a * a, axis=-1, keepdims=True) + NORM_EPS) * gain)
    return jnp.concatenate(outs, axis=-1) if len(outs) > 1 else outs[0]


def _q_kernel(a_ref, w_ref, n_ref, o_ref):
    acc = jnp.dot(a_ref[0], w_ref[...], preferred_element_type=F32)
    o_ref[0] = (_head_rms(acc, n_ref[...]) * (HEAD_DIM ** -0.5)).astype(o_ref.dtype)


def _k_kernel(a_ref, w_ref, n_ref, o_ref, ob_ref):
    acc = jnp.dot(a_ref[0], w_ref[...], preferred_element_type=F32)
    k = _head_rms(acc, n_ref[...])
    o_ref[0] = k
    ob_ref[0] = k.astype(ob_ref.dtype)


def _v_kernel(a_ref, w_ref, o_ref, ob_ref):
    acc = jnp.dot(a_ref[0], w_ref[...], preferred_element_type=F32)
    o_ref[0] = acc
    ob_ref[0] = acc.astype(ob_ref.dtype)


def _mm_specs(b, t, k, n, col0, tm_pref=1024, tn_pref=512):
    tm = _tile(t, tm_pref)
    tn = _tile(n, tn_pref)
    grid = (b, t // tm, n // tn)
    a_spec = pl.BlockSpec((1, tm, k), lambda bi, i, j: (bi, i, 0))
    w_spec = pl.BlockSpec((k, tn), lambda bi, i, j: (0, j + col0 // tn))
    o_spec = pl.BlockSpec((1, tm, tn), lambda bi, i, j: (bi, i, j))
    return tm, tn, grid, a_spec, w_spec, o_spec


_MM_SEM = ("arbitrary", "arbitrary", "arbitrary")


def _glu_proj(gl, w, bias, x, gate):
    b, t, d = gl.shape
    tm, tn, grid, a_spec, w_spec, o_spec = _mm_specs(b, t, d, d, 0)
    return pl.pallas_call(
        _glu_kernel, grid=grid,
        in_specs=[a_spec, w_spec, pl.BlockSpec((1, tn), lambda bi, i, j: (0, j)), o_spec, o_spec,
                  pl.BlockSpec((1, 1, tn), lambda bi, i, j: (bi, 0, j))],
        out_specs=o_spec, out_shape=jax.ShapeDtypeStruct((b, t, d), F32),
        compiler_params=_cparams(_MM_SEM), name="glu_proj",
    )(gl, w, bias.reshape(1, d), gl, x, gate)


def _out_proj(a, w, x, gate):
    b, t, d = a.shape
    tm, tn, grid, a_spec, w_spec, o_spec = _mm_specs(b, t, d, d, 0)
    return pl.pallas_call(
        _proj_res_kernel, grid=grid,
        in_specs=[a_spec, w_spec, o_spec, pl.BlockSpec((1, 1, tn), lambda bi, i, j: (bi, 0, j))],
        out_specs=o_spec, out_shape=jax.ShapeDtypeStruct((b, t, d), F32),
        compiler_params=_cparams(_MM_SEM), name="out_proj",
    )(a, w, x, gate)


def _qkv_proj(hn, w_qkv, q_norm, k_norm):
    b, t, d = hn.shape
    nspec = pl.BlockSpec((1, HEAD_DIM), lambda bi, i, j: (0, 0))
    tm, tn, grid, a_spec, wq_spec, o_spec = _mm_specs(b, t, d, d, 0)
    q = pl.pallas_call(
        _q_kernel, grid=grid, in_specs=[a_spec, wq_spec, nspec], out_specs=o_spec,
        out_shape=jax.ShapeDtypeStruct((b, t, d), BF16),
        compiler_params=_cparams(_MM_SEM), name="q_proj",
    )(hn, w_qkv, q_norm.reshape(1, HEAD_DIM))
    wk_spec = _mm_specs(b, t, d, d, d)[4]
    k, kb = pl.pallas_call(
        _k_kernel, grid=grid, in_specs=[a_spec, wk_spec, nspec], out_specs=[o_spec, o_spec],
        out_shape=[jax.ShapeDtypeStruct((b, t, d), F32), jax.ShapeDtypeStruct((b, t, d), BF16)],
        compiler_params=_cparams(_MM_SEM), name="k_proj",
    )(hn, w_qkv, k_norm.reshape(1, HEAD_DIM))
    wv_spec = _mm_specs(b, t, d, d, 2 * d)[4]
    v, vb = pl.pallas_call(
        _v_kernel, grid=grid, in_specs=[a_spec, wv_spec], out_specs=[o_spec, o_spec],
        out_shape=[jax.ShapeDtypeStruct((b, t, d), F32), jax.ShapeDtypeStruct((b, t, d), BF16)],
        compiler_params=_cparams(_MM_SEM), name="v_proj",
    )(hn, w_qkv)
    return q, k, kb, v, vb


def _attn_kernel(qi_ref, ki_ref, last_ref, slope_ref, lam_ref, q_ref, k_ref, v_ref, sub_ref, o_ref,
                 m1_ref, l1_ref, a1_ref, m2_ref, l2_ref, a2_ref, *, tq, tk, past, out_scale):
    h = pl.program_id(1)
    step = pl.program_id(2)
    qi = qi_ref[step]
    ki = ki_ref[step]
    slope = slope_ref[h]
    q0 = past + qi * tq
    k0 = ki * tk

    @pl.when(ki == 0)
    def _():
        for m_ref, l_ref, a_ref in ((m1_ref, l1_ref, a1_ref), (m2_ref, l2_ref, a2_ref)):
            m_ref[...] = jnp.full(m_ref.shape, NEG_INF, F32)
            l_ref[...] = jnp.zeros(l_ref.shape, F32)
            a_ref[...] = jnp.zeros(a_ref.shape, F32)

    def update(bias, visible):
        v = v_ref[0]
        for idx, (m_ref, l_ref, a_ref) in enumerate(((m1_ref, l1_ref, a1_ref), (m2_ref, l2_ref, a2_ref))):
            qm = q_ref[0, :, idx * HEAD_DIM:(idx + 1) * HEAD_DIM]
            km = k_ref[0, :, idx * HEAD_DIM:(idx + 1) * HEAD_DIM]
            s = lax.dot_general(qm, km, (((1,), (1,)), ((), ())), preferred_element_type=F32) + bias
            if visible is not None:
                s = jnp.where(visible, s, NEG_INF)
            m_old = m_ref[...]
            m_new = jnp.maximum(m_old, jnp.max(s, axis=-1, keepdims=True))
            alpha = jnp.exp(m_old - m_new)
            p = jnp.exp(s - m_new)
            l_ref[...] = alpha * l_ref[...] + jnp.sum(p, axis=-1, keepdims=True)
            a_ref[...] = alpha * a_ref[...] + jnp.dot(p.astype(v.dtype), v, preferred_element_type=F32)
            m_ref[...] = m_new

    strictly_past = (k0 + tk - 1) <= q0

    @pl.when(strictly_past)
    def _():
        kpos = k0 + lax.broadcasted_iota(jnp.int32, (1, tk), 1)
        update(slope * (kpos - q0).astype(F32), None)

    @pl.when(jnp.logical_not(strictly_past))
    def _():
        qpos = q0 + lax.broadcasted_iota(jnp.int32, (tq, tk), 0)
        kpos = k0 + lax.broadcasted_iota(jnp.int32, (tq, tk), 1)
        bias = slope * ((qpos - q0) - jnp.abs(qpos - kpos)).astype(F32)
        update(bias, (kpos // CHUNK) <= (qpos // CHUNK))

    @pl.when(last_ref[step] == 1)
    def _():
        o = a1_ref[...] / l1_ref[...] - lam_ref[0] * (a2_ref[...] / l2_ref[...])
        o = o * lax.rsqrt(jnp.mean(o * o, axis=-1, keepdims=True) + NORM_EPS) * sub_ref[...]
        o_ref[0] = (o * out_scale).astype(o_ref.dtype)


def _attn_tables(tq_len, tk_len, past, tq, tk):
    qi_l, ki_l, last_l = [], [], []
    for qi in range(tq_len // tq):
        q_last_chunk = (past + qi * tq + tq - 1) // CHUNK
        ks = [ki for ki in range(tk_len // tk) if (ki * tk) // CHUNK <= q_last_chunk]
        for ki in ks:
            qi_l.append(qi)
            ki_l.append(ki)
            last_l.append(1 if ki == ks[-1] else 0)
    return (np.asarray(qi_l, np.int32), np.asarray(ki_l, np.int32), np.asarray(last_l, np.int32))


def _diff_attention(q, k, v, lam, subln, past, lam_init, tq, tk):
    b, tq_len, d = q.shape
    tk_len = k.shape[1]
    n_heads = d // (2 * HEAD_DIM)
    hw = 2 * HEAD_DIM
    qi_t, ki_t, last_t = _attn_tables(tq_len, tk_len, past, tq, tk)
    slopes = jnp.asarray(2.0 ** (-8.0 * np.arange(1, n_heads + 1, dtype=np.float32) / n_heads), F32)
    grid_spec = pltpu.PrefetchScalarGridSpec(
        num_scalar_prefetch=5,
        grid=(b, n_heads, len(qi_t)),
        in_specs=[pl.BlockSpec((1, tq, hw), lambda bi, h, s, qi, ki, *_: (bi, qi[s], h)),
                  pl.BlockSpec((1, tk, hw), lambda bi, h, s, qi, ki, *_: (bi, ki[s], h)),
                  pl.BlockSpec((1, tk, hw), lambda bi, h, s, qi, ki, *_: (bi, ki[s], h)),
                  pl.BlockSpec((1, hw), lambda bi, h, s, *_: (0, 0))],
        out_specs=pl.BlockSpec((1, tq, hw), lambda bi, h, s, qi, ki, *_: (bi, qi[s], h)),
        scratch_shapes=[pltpu.VMEM((tq, 1), F32), pltpu.VMEM((tq, 1), F32), pltpu.VMEM((tq, hw), F32),
                        pltpu.VMEM((tq, 1), F32), pltpu.VMEM((tq, 1), F32), pltpu.VMEM((tq, hw), F32)],
    )
    return pl.pallas_call(
        functools.partial(_attn_kernel, tq=tq, tk=tk, past=past, out_scale=1.0 - lam_init),
        grid_spec=grid_spec,
        out_shape=jax.ShapeDtypeStruct((b, tq_len, d), BF16),
        compiler_params=_cparams(("arbitrary", "arbitrary", "arbitrary")), name="diff_attn",
    )(jnp.asarray(qi_t), jnp.asarray(ki_t), jnp.asarray(last_t), slopes, lam.reshape(1).astype(F32),
      q, k, v, subln.reshape(1, hw).astype(F32))


def _moe_kernel(be_ref, nu_ref, xs_ref, w1_ref, w3_ref, w2_ref, o_ref):
    blk = pl.program_id(1)

    @pl.when(blk < nu_ref[0])
    def _():
        x = xs_ref[...]
        h1 = jnp.dot(x, w1_ref[0], preferred_element_type=F32)
        h3 = jnp.dot(x, w3_ref[0], preferred_element_type=F32)
        hid = (h1 * jax.nn.sigmoid(h1)) * h3
        o_ref[0] = jnp.dot(hid.astype(BF16), w2_ref[0], preferred_element_type=F32)

    @pl.when(blk >= nu_ref[0])
    def _():
        o_ref[...] = jnp.zeros(o_ref.shape, o_ref.dtype)


def _route(logits, router_b):
    n, e = logits.shape
    per = e // N_EXPERT_GROUPS
    scores = jax.nn.sigmoid(logits)
    sel_g = (scores + router_b.astype(F32)).reshape(n, N_EXPERT_GROUPS, per)
    group_score = lax.top_k(sel_g, 2)[0].sum(-1)
    g_idx = jnp.argmax(group_score, axis=-1).astype(jnp.int32)
    in_group = jnp.take_along_axis(sel_g, g_idx[:, None, None], axis=1)[:, 0]
    _, local = lax.top_k(in_group, TOP_K)
    expert_idx = (g_idx[:, None] * per + local).astype(jnp.int32)
    w = jnp.take_along_axis(scores, expert_idx, axis=1)
    return expert_idx, w / jnp.sum(w, axis=-1, keepdims=True)


def _moe(hn, logits, router_b, w1, w3, w2, bm):
    n, d = hn.shape
    e, _, f = w1.shape
    expert_idx, gate = _route(logits, router_b)
    n_assign = n * TOP_K
    flat_e = expert_idx.reshape(n_assign)
    onehot = (flat_e[:, None] == jnp.arange(e, dtype=jnp.int32)[None, :]).astype(jnp.int32)
    csum = jnp.cumsum(onehot, axis=0)
    counts = csum[-1]
    rank = jnp.take_along_axis(csum, flat_e[:, None], axis=1)[:, 0] - 1
    padded = (counts + bm - 1) // bm * bm
    pad_end = jnp.cumsum(padded)
    pad_start = pad_end - padded
    dest = pad_start[flat_e] + rank
    n_blocks = -(-n_assign // bm) + e
    slot_tok = jnp.full((n_blocks * bm,), n, jnp.int32).at[dest].set(
        jnp.arange(n_assign, dtype=jnp.int32) // TOP_K)
    xs = jnp.concatenate([hn, jnp.zeros((1, d), hn.dtype)])[slot_tok]
    block_e = jnp.minimum(
        jnp.searchsorted(pad_end, jnp.arange(n_blocks, dtype=jnp.int32) * bm, side='right'),
        e - 1).astype(jnp.int32)
    n_used = (pad_end[-1] // bm).astype(jnp.int32).reshape(1)
    fs = 2 if f % 256 == 0 else 1
    fh = f // fs
    grid_spec = pltpu.PrefetchScalarGridSpec(
        num_scalar_prefetch=2,
        grid=(fs, n_blocks),
        in_specs=[pl.BlockSpec((bm, d), lambda fi, bi, be, nu: (bi, 0)),
                  pl.BlockSpec((1, d, fh), lambda fi, bi, be, nu: (be[bi], 0, fi)),
                  pl.BlockSpec((1, d, fh), lambda fi, bi, be, nu: (be[bi], 0, fi)),
                  pl.BlockSpec((1, fh, d), lambda fi, bi, be, nu: (be[bi], fi, 0))],
        out_specs=pl.BlockSpec((1, bm, d), lambda fi, bi, be, nu: (fi, bi, 0)),
    )
    ys = pl.pallas_call(
        _moe_kernel, grid_spec=grid_spec,
        out_shape=jax.ShapeDtypeStruct((fs, n_blocks * bm, d), F32),
        compiler_params=_cparams(("arbitrary", "arbitrary")), name="moe_experts",
    )(block_e, n_used, xs, w1, w3, w2)
    picked = ys[:, dest].sum(0).reshape(n, TOP_K, d)
    return jnp.sum(picked * gate[:, :, None], axis=1)


def kernel(x_prompt, x_sample, c_prompt, c_sample, state_ssm_re, state_ssm_im, cache_k, cache_v, ada_w, ada_b, norm_mix, norm_ffn, ssm_a_re, ssm_a_im, ssm_log_dt, ssm_b_re, ssm_b_im, ssm_c_re, ssm_c_im, ssm_d, ssm_glu_w, ssm_glu_b, attn_w_qkv, attn_w_o, attn_q_norm, attn_k_norm, attn_lambda, attn_subln, router_w, router_b, moe_w1, moe_w3, moe_w2):
    depth, d = norm_mix.shape
    n_mixers = 2
    bp, bs = x_prompt.shape[0], x_sample.shape[0]
    assert bp == 1 and bs == SUBLANES, "the S5 scan maps one long sequence or eight sequences onto sublanes"
    n_heads = d // (2 * HEAD_DIM)
    g, p = ssm_a_re.shape[1:]
    past_len = cache_k.shape[2]

    r_all = -(-(bp + bs) // SUBLANES) * SUBLANES
    c_all = jnp.concatenate([c_prompt, c_sample, jnp.zeros((r_all - bp - bs, d), F32)])
    mod_all = _ada_mod(c_all, ada_w, ada_b).reshape(depth, r_all, 6, d)

    glu_w = ssm_glu_w.astype(BF16)
    w_qkv = attn_w_qkv.astype(BF16)
    w_o = attn_w_o.astype(BF16)
    w1, w3, w2 = moe_w1.astype(BF16), moe_w3.astype(BF16), moe_w2.astype(BF16)

    def trunk(x, row0, h0_re, h0_im, past_k, past_v, chained, bm):
        b, t, _ = x.shape
        new_re, new_im, new_k, new_v = [], [], [], []
        for i in range(depth):
            mod = mod_all[i, row0:row0 + b][:, :, None, :]
            shift1, scale1, gate1, shift2, scale2, gate2 = (mod[:, m] for m in range(6))
            j = i // n_mixers
            hn = _norm_mod(x, norm_mix[i], scale1, shift1)
            if i % n_mixers == 0:
                seg_len = b * (t // S5_L) // SUBLANES
                prm = _s5_params(ssm_a_re[j], ssm_a_im[j], ssm_log_dt[j], ssm_b_re[j], ssm_b_im[j],
                                 ssm_c_re[j], ssm_c_im[j], ssm_d[j], seg_len)
                if chained:
                    pad = jnp.zeros((SUBLANES - 1, g * p), F32)
                    i_re = jnp.concatenate([h0_re[j].reshape(1, g * p), pad])
                    i_im = jnp.concatenate([h0_im[j].reshape(1, g * p), pad])
                else:
                    i_re, i_im = h0_re[j].reshape(b, g * p), h0_im[j].reshape(b, g * p)
                gl, f_re, f_im = _s5_mixer(hn, i_re, i_im, prm, chained)
                if chained:
                    f_re, f_im = f_re[SUBLANES - 1:], f_im[SUBLANES - 1:]
                new_re.append(f_re.reshape(b, g, p))
                new_im.append(f_im.reshape(b, g, p))
                x = _glu_proj(gl, glu_w[j], ssm_glu_b[j], x, gate1)
            else:
                lam_init = 0.8 - 0.6 * math.exp(-0.3 * i)
                q, k, kb, v, vb = _qkv_proj(hn, w_qkv[j], attn_q_norm[j], attn_k_norm[j])
                new_k.append(k.reshape(b, t, n_heads, 2, HEAD_DIM))
                new_v.append(v.reshape(b, t, n_heads, 2 * HEAD_DIM))
                lp = attn_lambda[j].astype(F32)
                lam = jnp.exp(jnp.sum(lp[0] * lp[1])) - jnp.exp(jnp.sum(lp[2] * lp[3])) + lam_init
                if past_k is None:
                    tq = tk = _tile(t, 512)
                    k_all, v_all, past = kb, vb, 0
                else:
                    past = past_k.shape[2]
                    tq = t
                    tk = _tile(past, 512)
                    fill = jnp.zeros((b, tk - t % tk if t % tk else 0, d), BF16)
                    k_all = jnp.concatenate([past_k[j].reshape(b, past, d).astype(BF16), kb, fill], axis=1)
                    v_all = jnp.concatenate([past_v[j].reshape(b, past, d).astype(BF16), vb, fill], axis=1)
                o = _diff_attention(q, k_all, v_all, lam, attn_subln[j], past, lam_init, tq, tk)
                x = _out_proj(o, w_o[j], x, gate1)
            hn, logits = _norm_mod(x, norm_ffn[i], scale2, shift2, router_w)
            y = _moe(hn.reshape(b * t, d), logits.reshape(b * t, -1), router_b, w1[i], w3[i], w2[i], bm)
            x = x + gate2 * y.reshape(b, t, d)
        return x, jnp.stack(new_re), jnp.stack(new_im), jnp.stack(new_k), jnp.stack(new_v)

    zero_state = jnp.zeros((state_ssm_re.shape[0], bp, g, p), F32)
    y_p, re_p, im_p, k_p, v_p = trunk(x_prompt, 0, zero_state, zero_state, None, None, True, 256)
    y_s, re_s, im_s, k_s, v_s = trunk(x_sample, bp, state_ssm_re, state_ssm_im, cache_k, cache_v, False, 128)
    return (y_p, y_s, re_p, im_p, k_p, v_p, re_s, im_s, k_s, v_s)
```

```python
import functools
import math

import numpy as np
import jax
import jax.numpy as jnp
from jax import lax
from jax.experimental import pallas as pl
from jax.experimental.pallas import tpu as pltpu

F32 = jnp.float32
BF16 = jnp.bfloat16
HIGHEST = lax.Precision.HIGHEST

NORM_EPS = 1e-6
CHUNK = 64
SSM_GROUP = 16
SSM_STATE = 64
S5_L = 16
HEAD_DIM = 128
N_EXPERT_GROUPS = 8
TOP_K = 2
NEG_INF = -1e30
LOG2E = 1.4426950408889634
ATTN_TQ = 512
ATTN_TK = 2048
SUBLANES = 8
VMEM_LIMIT = 52 * 1024 * 1024


def _cparams(sem):
    return pltpu.CompilerParams(dimension_semantics=sem, vmem_limit_bytes=VMEM_LIMIT)


def _tile(n, pref):
    if n <= pref:
        return n
    t = pref
    while n % t:
        t //= 2
    return t


def _ada_kernel(c_ref, w_ref, b_ref, o_ref):
    c = c_ref[...]
    s = c * jax.nn.sigmoid(c)
    o_ref[0] = jnp.dot(s, w_ref[0], preferred_element_type=F32, precision=HIGHEST) + b_ref[0]


def _ada_mod(c_all, ada_w, ada_b):
    depth, d, n6 = ada_w.shape
    r = c_all.shape[0]
    tn = _tile(n6, 512)
    return pl.pallas_call(
        _ada_kernel,
        grid=(depth, n6 // tn),
        in_specs=[pl.BlockSpec((r, d), lambda i, j: (0, 0)),
                  pl.BlockSpec((1, d, tn), lambda i, j: (i, 0, j)),
                  pl.BlockSpec((1, 1, tn), lambda i, j: (i, 0, j))],
        out_specs=pl.BlockSpec((1, r, tn), lambda i, j: (i, 0, j)),
        out_shape=jax.ShapeDtypeStruct((depth, r, n6), F32),
        compiler_params=_cparams(("arbitrary", "arbitrary")),
        name="ada_mod",
    )(c_all, ada_w, ada_b.reshape(depth, 1, n6))


def _normmod_kernel(x_ref, g_ref, sc_ref, sh_ref, o_ref):
    x = x_ref[0]
    y = x * lax.rsqrt(jnp.mean(x * x, axis=-1, keepdims=True) + NORM_EPS)
    o_ref[0] = (y * g_ref[...] * (1.0 + sc_ref[0]) + sh_ref[0]).astype(o_ref.dtype)


def _normmod_router_kernel(x_ref, g_ref, sc_ref, sh_ref, rw_ref, o_ref, lg_ref):
    x = x_ref[0]
    y = x * lax.rsqrt(jnp.mean(x * x, axis=-1, keepdims=True) + NORM_EPS)
    hn = (y * g_ref[...] * (1.0 + sc_ref[0]) + sh_ref[0]).astype(o_ref.dtype)
    o_ref[0] = hn
    lg_ref[0] = jnp.dot(hn, rw_ref[...], preferred_element_type=F32)


def _norm_mod(x, g, scale, shift, router_w=None):
    b, t, d = x.shape
    tm = _tile(t, 256)
    xspec = pl.BlockSpec((1, tm, d), lambda i, j: (i, j, 0))
    vspec = pl.BlockSpec((1, 1, d), lambda i, j: (i, 0, 0))
    gspec = pl.BlockSpec((1, d), lambda i, j: (0, 0))
    if router_w is None:
        return pl.pallas_call(
            _normmod_kernel, grid=(b, t // tm),
            in_specs=[xspec, gspec, vspec, vspec], out_specs=xspec,
            out_shape=jax.ShapeDtypeStruct((b, t, d), BF16),
            compiler_params=_cparams(("arbitrary", "arbitrary")), name="norm_mod",
        )(x, g.reshape(1, d), scale, shift)
    e = router_w.shape[1]
    return pl.pallas_call(
        _normmod_router_kernel, grid=(b, t // tm),
        in_specs=[xspec, gspec, vspec, vspec, pl.BlockSpec((d, e), lambda i, j: (0, 0))],
        out_specs=[xspec, pl.BlockSpec((1, tm, e), lambda i, j: (i, j, 0))],
        out_shape=[jax.ShapeDtypeStruct((b, t, d), BF16), jax.ShapeDtypeStruct((b, t, e), F32)],
        compiler_params=_cparams(("arbitrary", "arbitrary")), name="norm_mod_router",
    )(x, g.reshape(1, d), scale, shift, router_w)


def _cmul(ar, ai, br, bi):
    return ar * br - ai * bi, ar * bi + ai * br


def _cpow(re, im, n):
    out_re, out_im = jnp.ones_like(re), jnp.zeros_like(im)
    while n:
        if n & 1:
            out_re, out_im = _cmul(out_re, out_im, re, im)
        re, im = _cmul(re, im, re, im)
        n >>= 1
    return out_re, out_im


def _s5_params(a_re, a_im, log_dt, b_re, b_im, c_re, c_im, d, seg_len):
    g, p = a_re.shape
    h = SSM_GROUP
    L = S5_L
    q = g // 2
    dt = jnp.exp(log_dt.astype(F32))[:, None]
    mag = jnp.exp(a_re * dt)
    abar_re = mag * jnp.cos(a_im * dt)
    abar_im = mag * jnp.sin(a_im * dt)
    den = a_re * a_re + a_im * a_im
    nr = abar_re - 1.0
    coef_re = (nr * a_re + abar_im * a_im) / den
    coef_im = (abar_im * a_re - nr * a_im) / den
    bbar_re = coef_re[..., None] * b_re - coef_im[..., None] * b_im
    bbar_im = coef_re[..., None] * b_im + coef_im[..., None] * b_re
    pw_re, pw_im = [jnp.ones_like(abar_re)], [jnp.zeros_like(abar_im)]
    for _ in range(L):
        r_, i_ = _cmul(pw_re[-1], pw_im[-1], abar_re, abar_im)
        pw_re.append(r_)
        pw_im.append(i_)
    pw_re = jnp.stack(pw_re)
    pw_im = jnp.stack(pw_im)
    w1_re = pw_re[:L, :, :, None] * bbar_re[None] - pw_im[:L, :, :, None] * bbar_im[None]
    w1_im = pw_re[:L, :, :, None] * bbar_im[None] + pw_im[:L, :, :, None] * bbar_re[None]
    k_tau = (jnp.einsum('gop,tgpi->tgio', c_re, w1_re, precision=HIGHEST)
             - jnp.einsum('gop,tgpi->tgio', c_im, w1_im, precision=HIGHEST))
    lag = np.arange(L)[None, :] - np.arange(L)[:, None]
    m5 = jnp.where((lag >= 0)[:, :, None, None, None], k_tau[np.maximum(lag, 0)], 0.0)
    m_mat = m5.transpose(2, 0, 3, 1, 4).reshape(g, L * h, L * h)
    ws_re = w1_re[::-1].transpose(1, 0, 3, 2).reshape(g, L * h, p)
    ws_im = w1_im[::-1].transpose(1, 0, 3, 2).reshape(g, L * h, p)
    z = jnp.zeros_like(ws_re[0::2])
    ws_top = jnp.concatenate([ws_re[0::2], z, ws_im[0::2], z], axis=-1)
    ws_bot = jnp.concatenate([z, ws_re[1::2], z, ws_im[1::2]], axis=-1)
    ws_pair = jnp.concatenate([ws_top, ws_bot], axis=1)
    x_re = c_re[None] * pw_re[1:, :, None, :] - c_im[None] * pw_im[1:, :, None, :]
    x_im = c_re[None] * pw_im[1:, :, None, :] + c_im[None] * pw_re[1:, :, None, :]
    wy_re = x_re.transpose(1, 3, 0, 2).reshape(g, p, L * h)
    wy_im = (-x_im).transpose(1, 3, 0, 2).reshape(g, p, L * h)
    zy = jnp.zeros_like(wy_re[0::2])
    wy_pair = jnp.concatenate([
        jnp.concatenate([wy_re[0::2], zy], axis=-1),
        jnp.concatenate([zy, wy_re[1::2]], axis=-1),
        jnp.concatenate([wy_im[0::2], zy], axis=-1),
        jnp.concatenate([zy, wy_im[1::2]], axis=-1)], axis=1)
    achunk_re, achunk_im = pw_re[L], pw_im[L]
    aseg_re, aseg_im = _cpow(achunk_re, achunk_im, seg_len)
    d_pair = jnp.broadcast_to(d.reshape(q, 2, 1, h), (q, 2, L, h)).reshape(q, 1, 2 * L * h)
    return dict(m=m_mat.astype(BF16), ws=ws_pair.astype(BF16), wy=wy_pair.astype(BF16),
                ac_re=achunk_re.reshape(q, 1, 2 * p), ac_im=achunk_im.reshape(q, 1, 2 * p),
                as_re=aseg_re.reshape(q, 1, 2 * p), as_im=aseg_im.reshape(q, 1, 2 * p), d=d_pair)


def _s5_local_kernel(u_ref, ws_ref, sre_ref, sim_ref, *, pb):
    for i in range(pb):
        s = jnp.dot(u_ref[i], ws_ref[i], preferred_element_type=F32)
        sre_ref[i] = s[:, :128]
        sim_ref[i] = s[:, 128:]


def _s5_scan_kernel(sre_ref, sim_ref, ac_re_ref, ac_im_ref, as_re_ref, as_im_ref, i_re_ref, i_im_ref,
                    hre_ref, him_ref, fre_ref, fim_ref, *, seg_len, chained, pb):
    shape = (SUBLANES, 128)
    ar = [jnp.broadcast_to(ac_re_ref[j], shape) for j in range(pb)]
    ai = [jnp.broadcast_to(ac_im_ref[j], shape) for j in range(pb)]

    def advance(i, carry):
        out = []
        for j in range(pb):
            hr, hi = carry[2 * j], carry[2 * j + 1]
            out.append(ar[j] * hr - ai[j] * hi + sre_ref[j, pl.ds(i, SUBLANES, stride=seg_len), :])
            out.append(ar[j] * hi + ai[j] * hr + sim_ref[j, pl.ds(i, SUBLANES, stride=seg_len), :])
        return tuple(out)

    def store_advance(i, carry):
        for j in range(pb):
            hre_ref[j, pl.ds(i, SUBLANES, stride=seg_len), :] = carry[2 * j]
            him_ref[j, pl.ds(i, SUBLANES, stride=seg_len), :] = carry[2 * j + 1]
        return advance(i, carry)

    start = []
    for j in range(pb):
        start += [i_re_ref[j], i_im_ref[j]]
    start = tuple(start)
    if chained:
        ends = lax.fori_loop(0, seg_len, advance, start)
        row = lax.broadcasted_iota(jnp.int32, shape, 0)
        fixed = []
        for j in range(pb):
            er, ei = ends[2 * j], ends[2 * j + 1]
            sr, si = as_re_ref[j], as_im_ref[j]
            tr, ti = er[0:1], ei[0:1]
            t_re, t_im = start[2 * j], start[2 * j + 1]
            for s in range(1, SUBLANES):
                if s > 1:
                    tr, ti = sr * tr - si * ti + er[s - 1:s], sr * ti + si * tr + ei[s - 1:s]
                t_re = jnp.where(row == s, jnp.broadcast_to(tr, shape), t_re)
                t_im = jnp.where(row == s, jnp.broadcast_to(ti, shape), t_im)
            fixed += [t_re, t_im]
        start = tuple(fixed)
    final = lax.fori_loop(0, seg_len, store_advance, start)
    for j in range(pb):
        fre_ref[j] = final[2 * j]
        fim_ref[j] = final[2 * j + 1]


def _s5_out_kernel(u_ref, m_ref, wy_ref, hre_ref, him_ref, d_ref, o_ref, *, pb):
    for i in range(pb):
        u = u_ref[i]
        hp = jnp.concatenate([hre_ref[i], him_ref[i]], axis=-1).astype(BF16)
        y = jnp.dot(hp, wy_ref[i], preferred_element_type=F32)
        y0 = jnp.dot(u[:, :256], m_ref[2 * i], preferred_element_type=F32)
        y1 = jnp.dot(u[:, 256:], m_ref[2 * i + 1], preferred_element_type=F32)
        y = y + jnp.concatenate([y0, y1], axis=-1) + d_ref[i] * u.astype(F32)
        o_ref[i] = jax.nn.gelu(y, approximate=True).astype(o_ref.dtype)


def _s5_mixer(hn, h0_re, h0_im, prm, chained):
    b, t, d = hn.shape
    g = d // SSM_GROUP
    q = g // 2
    L = S5_L
    nc = t // L
    r = b * nc
    w = g * SSM_STATE
    seg_len = r // SUBLANES
    lw = 2 * L * SSM_GROUP
    u = hn.reshape(b, nc, L, q, 2, SSM_GROUP).transpose(3, 0, 1, 4, 2, 5).reshape(q, r, lw)
    pb = _tile(q, 4)
    blk = pl.BlockSpec((pb, r, 128), lambda i: (i, 0, 0))
    s_re, s_im = pl.pallas_call(
        functools.partial(_s5_local_kernel, pb=pb), grid=(q // pb,),
        in_specs=[pl.BlockSpec((pb, r, lw), lambda i: (i, 0, 0)),
                  pl.BlockSpec((pb, lw, 256), lambda i: (i, 0, 0))],
        out_specs=[blk, blk],
        out_shape=[jax.ShapeDtypeStruct((q, r, 128), F32)] * 2,
        compiler_params=_cparams(("arbitrary",)), name="s5_local",
    )(u, prm['ws'])
    rowspec = pl.BlockSpec((pb, 1, 128), lambda i: (i, 0, 0))
    st = pl.BlockSpec((pb, SUBLANES, 128), lambda i: (i, 0, 0))
    to_pairs = lambda a: a.reshape(SUBLANES, q, 128).transpose(1, 0, 2)
    h_re, h_im, f_re, f_im = pl.pallas_call(
        functools.partial(_s5_scan_kernel, seg_len=seg_len, chained=chained, pb=pb), grid=(q // pb,),
        in_specs=[blk, blk, rowspec, rowspec, rowspec, rowspec, st, st],
        out_specs=[blk, blk, st, st],
        out_shape=[jax.ShapeDtypeStruct((q, r, 128), F32)] * 2
        + [jax.ShapeDtypeStruct((q, SUBLANES, 128), F32)] * 2,
        compiler_params=_cparams(("arbitrary",)), name="s5_scan",
    )(s_re, s_im, prm['ac_re'], prm['ac_im'], prm['as_re'], prm['as_im'], to_pairs(h0_re), to_pairs(h0_im))
    gl = pl.pallas_call(
        functools.partial(_s5_out_kernel, pb=pb), grid=(q // pb,),
        in_specs=[pl.BlockSpec((pb, r, lw), lambda i: (i, 0, 0)),
                  pl.BlockSpec((2 * pb, 256, 256), lambda i: (i, 0, 0)),
                  pl.BlockSpec((pb, 256, lw), lambda i: (i, 0, 0)),
                  blk, blk,
                  pl.BlockSpec((pb, 1, lw), lambda i: (i, 0, 0))],
        out_specs=pl.BlockSpec((pb, r, lw), lambda i: (i, 0, 0)),
        out_shape=jax.ShapeDtypeStruct((q, r, lw), BF16),
        compiler_params=_cparams(("arbitrary",)), name="s5_out",
    )(u, prm['m'], prm['wy'], h_re, h_im, prm['d'])
    gl = gl.reshape(q, b, nc, 2, L, SSM_GROUP).transpose(1, 2, 4, 0, 3, 5).reshape(b, t, d)
    from_pairs = lambda a: a.transpose(1, 0, 2).reshape(SUBLANES, w)
    return gl, from_pairs(f_re), from_pairs(f_im)


def _glu_kernel(g_ref, w_ref, b_ref, gt_ref, x_ref, gate_ref, o_ref):
    acc = jnp.dot(g_ref[0], w_ref[...], preferred_element_type=F32) + b_ref[...]
    o_ref[0] = x_ref[0] + gate_ref[0] * (gt_ref[0].astype(F32) * jax.nn.sigmoid(acc))


def _proj_res_kernel(a_ref, w_ref, x_ref, gate_ref, o_ref):
    acc = jnp.dot(a_ref[0], w_ref[...], preferred_element_type=F32)
    o_ref[0] = x_ref[0] + gate_ref[0] * acc


def _head_rms(acc, gain):
    outs = []
    for s in range(acc.shape[1] // HEAD_DIM):
        a = acc[:, s * HEAD_DIM:(s + 1) * HEAD_DIM]
        outs.append(a * lax.rsqrt(jnp.mean(a * a, axis=-1, keepdims=True) + NORM_EPS) * gain)
    return jnp.concatenate(outs, axis=-1) if len(outs) > 1 else outs[0]


def _q_kernel(a_ref, w_ref, n_ref, o_ref):
    acc = jnp.dot(a_ref[0], w_ref[...], preferred_element_type=F32)
    o_ref[0] = (_head_rms(acc, n_ref[...]) * (LOG2E * HEAD_DIM ** -0.5)).astype(o_ref.dtype)


def _k_kernel(a_ref, w_ref, n_ref, o_ref, ob_ref):
    acc = jnp.dot(a_ref[0], w_ref[...], preferred_element_type=F32)
    k = _head_rms(acc, n_ref[...])
    o_ref[0] = k
    ob_ref[0] = k.astype(ob_ref.dtype)


def _v_kernel(a_ref, w_ref, o_ref, ob_ref):
    acc = jnp.dot(a_ref[0], w_ref[...], preferred_element_type=F32)
    o_ref[0] = acc
    ob_ref[0] = acc.astype(ob_ref.dtype)


def _mm_specs(b, t, k, n, col0, tm_pref=1024, tn_pref=512):
    tm = _tile(t, tm_pref)
    tn = _tile(n, tn_pref)
    grid = (b, t // tm, n // tn)
    a_spec = pl.BlockSpec((1, tm, k), lambda bi, i, j: (bi, i, 0))
    w_spec = pl.BlockSpec((k, tn), lambda bi, i, j: (0, j + col0 // tn))
    o_spec = pl.BlockSpec((1, tm, tn), lambda bi, i, j: (bi, i, j))
    return tm, tn, grid, a_spec, w_spec, o_spec


_MM_SEM = ("arbitrary", "arbitrary", "arbitrary")


def _glu_proj(gl, w, bias, x, gate):
    b, t, d = gl.shape
    tm, tn, grid, a_spec, w_spec, o_spec = _mm_specs(b, t, d, d, 0)
    return pl.pallas_call(
        _glu_kernel, grid=grid,
        in_specs=[a_spec, w_spec, pl.BlockSpec((1, tn), lambda bi, i, j: (0, j)), o_spec, o_spec,
                  pl.BlockSpec((1, 1, tn), lambda bi, i, j: (bi, 0, j))],
        out_specs=o_spec, out_shape=jax.ShapeDtypeStruct((b, t, d), F32),
        compiler_params=_cparams(_MM_SEM), name="glu_proj",
    )(gl, w, bias.reshape(1, d), gl, x, gate)


def _out_proj(a, w, x, gate):
    b, t, d = a.shape
    tm, tn, grid, a_spec, w_spec, o_spec = _mm_specs(b, t, d, d, 0)
    return pl.pallas_call(
        _proj_res_kernel, grid=grid,
        in_specs=[a_spec, w_spec, o_spec, pl.BlockSpec((1, 1, tn), lambda bi, i, j: (bi, 0, j))],
        out_specs=o_spec, out_shape=jax.ShapeDtypeStruct((b, t, d), F32),
        compiler_params=_cparams(_MM_SEM), name="out_proj",
    )(a, w, x, gate)


def _qkv_proj(hn, w_qkv, q_norm, k_norm):
    b, t, d = hn.shape
    nspec = pl.BlockSpec((1, HEAD_DIM), lambda bi, i, j: (0, 0))
    tm, tn, grid, a_spec, wq_spec, o_spec = _mm_specs(b, t, d, d, 0)
    q = pl.pallas_call(
        _q_kernel, grid=grid, in_specs=[a_spec, wq_spec, nspec], out_specs=o_spec,
        out_shape=jax.ShapeDtypeStruct((b, t, d), BF16),
        compiler_params=_cparams(_MM_SEM), name="q_proj",
    )(hn, w_qkv, q_norm.reshape(1, HEAD_DIM))
    wk_spec = _mm_specs(b, t, d, d, d)[4]
    k, kb = pl.pallas_call(
        _k_kernel, grid=grid, in_specs=[a_spec, wk_spec, nspec], out_specs=[o_spec, o_spec],
        out_shape=[jax.ShapeDtypeStruct((b, t, d), F32), jax.ShapeDtypeStruct((b, t, d), BF16)],
        compiler_params=_cparams(_MM_SEM), name="k_proj",
    )(hn, w_qkv, k_norm.reshape(1, HEAD_DIM))
    wv_spec = _mm_specs(b, t, d, d, 2 * d)[4]
    v, vb = pl.pallas_call(
        _v_kernel, grid=grid, in_specs=[a_spec, wv_spec], out_specs=[o_spec, o_spec],
        out_shape=[jax.ShapeDtypeStruct((b, t, d), F32), jax.ShapeDtypeStruct((b, t, d), BF16)],
        compiler_params=_cparams(_MM_SEM), name="v_proj",
    )(hn, w_qkv)
    return q, k, kb, v, vb


def _attn_kernel(qi_ref, ki_ref, last_ref, slope_ref, lam_ref, q_ref, k_ref, v_ref, sub_ref, o_ref,
                 m1_ref, l1_ref, a1_ref, m2_ref, l2_ref, a2_ref, *, tq, tk, past, out_scale):
    h = pl.program_id(1)
    step = pl.program_id(2)
    qi = qi_ref[step]
    ki = ki_ref[step]
    slope = slope_ref[h]
    q0 = past + qi * tq
    k0 = ki * tk

    @pl.when(ki == 0)
    def _():
        for m_ref, l_ref, a_ref in ((m1_ref, l1_ref, a1_ref), (m2_ref, l2_ref, a2_ref)):
            m_ref[...] = jnp.full(m_ref.shape, NEG_INF, F32)
            l_ref[...] = jnp.zeros(l_ref.shape, F32)
            a_ref[...] = jnp.zeros(a_ref.shape, F32)

    def update(bias, visible):
        v = v_ref[0]
        for idx, (m_ref, l_ref, a_ref) in enumerate(((m1_ref, l1_ref, a1_ref), (m2_ref, l2_ref, a2_ref))):
            qm = q_ref[0, :, idx * HEAD_DIM:(idx + 1) * HEAD_DIM]
            km = k_ref[0, :, idx * HEAD_DIM:(idx + 1) * HEAD_DIM]
            s = lax.dot_general(qm, km, (((1,), (1,)), ((), ())), preferred_element_type=F32) + bias
            if visible is not None:
                s = jnp.where(visible, s, NEG_INF)
            m_old = m_ref[...]
            m_new = jnp.maximum(m_old, jnp.max(s, axis=-1, keepdims=True))
            alpha = jnp.exp2(m_old - m_new)
            p = jnp.exp2(s - m_new)
            l_ref[...] = alpha * l_ref[...] + jnp.sum(p, axis=-1, keepdims=True)
            a_ref[...] = alpha * a_ref[...] + jnp.dot(p.astype(v.dtype), v, preferred_element_type=F32)
            m_ref[...] = m_new

    strictly_past = (k0 + tk - 1) <= q0

    @pl.when(strictly_past)
    def _():
        kpos = k0 + lax.broadcasted_iota(jnp.int32, (1, tk), 1)
        update(slope * (kpos - q0).astype(F32), None)

    @pl.when(jnp.logical_not(strictly_past))
    def _():
        qpos = q0 + lax.broadcasted_iota(jnp.int32, (tq, tk), 0)
        kpos = k0 + lax.broadcasted_iota(jnp.int32, (tq, tk), 1)
        bias = slope * ((qpos - q0) - jnp.abs(qpos - kpos)).astype(F32)
        update(bias, (kpos // CHUNK) <= (qpos // CHUNK))

    @pl.when(last_ref[step] == 1)
    def _():
        o = a1_ref[...] / l1_ref[...] - lam_ref[0] * (a2_ref[...] / l2_ref[...])
        o = o * lax.rsqrt(jnp.mean(o * o, axis=-1, keepdims=True) + NORM_EPS) * sub_ref[...]
        o_ref[0] = (o * out_scale).astype(o_ref.dtype)


def _attn_tables(tq_len, tk_len, past, tq, tk):
    qi_l, ki_l, last_l = [], [], []
    for qi in range(tq_len // tq):
        q_last_chunk = (past + qi * tq + tq - 1) // CHUNK
        ks = [ki for ki in range(tk_len // tk) if (ki * tk) // CHUNK <= q_last_chunk]
        for ki in ks:
            qi_l.append(qi)
            ki_l.append(ki)
            last_l.append(1 if ki == ks[-1] else 0)
    return (np.asarray(qi_l, np.int32), np.asarray(ki_l, np.int32), np.asarray(last_l, np.int32))


def _diff_attention(q, k, v, lam, subln, past, lam_init, tq, tk):
    b, tq_len, d = q.shape
    tk_len = k.shape[1]
    n_heads = d // (2 * HEAD_DIM)
    hw = 2 * HEAD_DIM
    qi_t, ki_t, last_t = _attn_tables(tq_len, tk_len, past, tq, tk)
    slopes = jnp.asarray(LOG2E * 2.0 ** (-8.0 * np.arange(1, n_heads + 1, dtype=np.float32) / n_heads), F32)
    grid_spec = pltpu.PrefetchScalarGridSpec(
        num_scalar_prefetch=5,
        grid=(b, n_heads, len(qi_t)),
        in_specs=[pl.BlockSpec((1, tq, hw), lambda bi, h, s, qi, ki, *_: (bi, qi[s], h)),
                  pl.BlockSpec((1, tk, hw), lambda bi, h, s, qi, ki, *_: (bi, ki[s], h)),
                  pl.BlockSpec((1, tk, hw), lambda bi, h, s, qi, ki, *_: (bi, ki[s], h)),
                  pl.BlockSpec((1, hw), lambda bi, h, s, *_: (0, 0))],
        out_specs=pl.BlockSpec((1, tq, hw), lambda bi, h, s, qi, ki, *_: (bi, qi[s], h)),
        scratch_shapes=[pltpu.VMEM((tq, 1), F32), pltpu.VMEM((tq, 1), F32), pltpu.VMEM((tq, hw), F32),
                        pltpu.VMEM((tq, 1), F32), pltpu.VMEM((tq, 1), F32), pltpu.VMEM((tq, hw), F32)],
    )
    return pl.pallas_call(
        functools.partial(_attn_kernel, tq=tq, tk=tk, past=past, out_scale=1.0 - lam_init),
        grid_spec=grid_spec,
        out_shape=jax.ShapeDtypeStruct((b, tq_len, d), BF16),
        compiler_params=_cparams(("arbitrary", "arbitrary", "arbitrary")), name="diff_attn",
    )(jnp.asarray(qi_t), jnp.asarray(ki_t), jnp.asarray(last_t), slopes, lam.reshape(1).astype(F32),
      q, k, v, subln.reshape(1, hw).astype(F32))


def _moe_kernel(be_ref, nu_ref, xs_ref, w1_ref, w3_ref, w2_ref, o_ref):
    blk = pl.program_id(0)

    @pl.when(blk < nu_ref[0])
    def _():
        x = xs_ref[...]
        h1 = jnp.dot(x, w1_ref[0, 0], preferred_element_type=F32)
        h3 = jnp.dot(x, w3_ref[0, 0], preferred_element_type=F32)
        hid = (h1 * jax.nn.sigmoid(h1)) * h3
        o_ref[...] = jnp.dot(hid.astype(BF16), w2_ref[0, 0], preferred_element_type=F32)

    @pl.when(blk >= nu_ref[0])
    def _():
        o_ref[...] = jnp.zeros(o_ref.shape, o_ref.dtype)


def _top2(x):
    col = lax.broadcasted_iota(jnp.int32, x.shape, x.ndim - 1)
    i0 = jnp.argmax(x, axis=-1).astype(jnp.int32)
    v0 = jnp.max(x, axis=-1)
    rest = jnp.where(col == i0[..., None], -jnp.inf, x)
    i1 = jnp.argmax(rest, axis=-1).astype(jnp.int32)
    v1 = jnp.max(rest, axis=-1)
    return (v0, v1), (i0, i1)


def _route(logits, router_b):
    n, e = logits.shape
    per = e // N_EXPERT_GROUPS
    scores = jax.nn.sigmoid(logits)
    sel_g = (scores + router_b.astype(F32)).reshape(n, N_EXPERT_GROUPS, per)
    (v0, v1), _ = _top2(sel_g)
    g_idx = jnp.argmax(v0 + v1, axis=-1).astype(jnp.int32)
    in_group = jnp.take_along_axis(sel_g, g_idx[:, None, None], axis=1)[:, 0]
    _, (l0, l1) = _top2(in_group)
    expert_idx = g_idx[:, None] * per + jnp.stack([l0, l1], axis=-1)
    w = jnp.take_along_axis(scores, expert_idx, axis=1)
    return expert_idx, w / jnp.sum(w, axis=-1, keepdims=True)


def _moe(hn, logits, router_b, w1, w3, w2, layer, bm):
    n, d = hn.shape
    e, f = w1.shape[1], w1.shape[3]
    expert_idx, gate = _route(logits, router_b)
    n_assign = n * TOP_K
    flat_e = expert_idx.reshape(n_assign)
    onehot = (flat_e[:, None] == jnp.arange(e, dtype=jnp.int32)[None, :]).astype(jnp.int32)
    csum = jnp.cumsum(onehot, axis=0)
    counts = csum[-1]
    rank = jnp.take_along_axis(csum, flat_e[:, None], axis=1)[:, 0] - 1
    padded = (counts + bm - 1) // bm * bm
    pad_end = jnp.cumsum(padded)
    pad_start = pad_end - padded
    dest = pad_start[flat_e] + rank
    n_blocks = -(-n_assign // bm) + e
    slot_tok = jnp.zeros((n_blocks * bm,), jnp.int32).at[dest].set(
        jnp.arange(n_assign, dtype=jnp.int32) // TOP_K)
    xs = hn[slot_tok]
    block_e = jnp.minimum(
        jnp.searchsorted(pad_end, jnp.arange(n_blocks, dtype=jnp.int32) * bm, side='right'),
        e - 1).astype(jnp.int32)
    n_used = (pad_end[-1] // bm).astype(jnp.int32).reshape(1)
    once = pl.Buffered(1)
    grid_spec = pltpu.PrefetchScalarGridSpec(
        num_scalar_prefetch=2,
        grid=(n_blocks,),
        in_specs=[pl.BlockSpec((bm, d), lambda bi, be, nu: (bi, 0)),
                  pl.BlockSpec((1, 1, d, f), lambda bi, be, nu: (layer, be[bi], 0, 0), pipeline_mode=once),
                  pl.BlockSpec((1, 1, d, f), lambda bi, be, nu: (layer, be[bi], 0, 0), pipeline_mode=once),
                  pl.BlockSpec((1, 1, f, d), lambda bi, be, nu: (layer, be[bi], 0, 0), pipeline_mode=once)],
        out_specs=pl.BlockSpec((bm, d), lambda bi, be, nu: (bi, 0)),
    )
    ys = pl.pallas_call(
        _moe_kernel, grid_spec=grid_spec,
        out_shape=jax.ShapeDtypeStruct((n_blocks * bm, d), F32),
        compiler_params=_cparams(("arbitrary",)), name="moe_experts",
    )(block_e, n_used, xs, w1, w3, w2)
    dest = dest.reshape(n, TOP_K)
    return gate[:, 0:1] * ys[dest[:, 0]] + gate[:, 1:2] * ys[dest[:, 1]]


def kernel(x_prompt, x_sample, c_prompt, c_sample, state_ssm_re, state_ssm_im, cache_k, cache_v, ada_w, ada_b, norm_mix, norm_ffn, ssm_a_re, ssm_a_im, ssm_log_dt, ssm_b_re, ssm_b_im, ssm_c_re, ssm_c_im, ssm_d, ssm_glu_w, ssm_glu_b, attn_w_qkv, attn_w_o, attn_q_norm, attn_k_norm, attn_lambda, attn_subln, router_w, router_b, moe_w1, moe_w3, moe_w2):
    depth, d = norm_mix.shape
    n_mixers = 2
    bp, bs = x_prompt.shape[0], x_sample.shape[0]
    assert bp == 1 and bs == SUBLANES, "the S5 scan maps one long sequence or eight sequences onto sublanes"
    n_heads = d // (2 * HEAD_DIM)
    g, p = ssm_a_re.shape[1:]
    past_len = cache_k.shape[2]

    r_all = -(-(bp + bs) // SUBLANES) * SUBLANES
    c_all = jnp.concatenate([c_prompt, c_sample, jnp.zeros((r_all - bp - bs, d), F32)])
    mod_all = _ada_mod(c_all, ada_w, ada_b).reshape(depth, r_all, 6, d)

    glu_w = ssm_glu_w.astype(BF16)
    w_qkv = attn_w_qkv.astype(BF16)
    w_o = attn_w_o.astype(BF16)
    w1, w3, w2 = moe_w1.astype(BF16), moe_w3.astype(BF16), moe_w2.astype(BF16)
    router_bf = router_w.astype(BF16)

    def trunk(x, row0, h0_re, h0_im, past_k, past_v, chained, bm):
        b, t, _ = x.shape
        new_re, new_im, new_k, new_v = [], [], [], []
        for i in range(depth):
            mod = mod_all[i, row0:row0 + b][:, :, None, :]
            shift1, scale1, gate1, shift2, scale2, gate2 = (mod[:, m] for m in range(6))
            j = i // n_mixers
            hn = _norm_mod(x, norm_mix[i], scale1, shift1)
            if i % n_mixers == 0:
                seg_len = b * (t // S5_L) // SUBLANES
                prm = _s5_params(ssm_a_re[j], ssm_a_im[j], ssm_log_dt[j], ssm_b_re[j], ssm_b_im[j],
                                 ssm_c_re[j], ssm_c_im[j], ssm_d[j], seg_len)
                if chained:
                    pad = jnp.zeros((SUBLANES - 1, g * p), F32)
                    i_re = jnp.concatenate([h0_re[j].reshape(1, g * p), pad])
                    i_im = jnp.concatenate([h0_im[j].reshape(1, g * p), pad])
                else:
                    i_re, i_im = h0_re[j].reshape(b, g * p), h0_im[j].reshape(b, g * p)
                gl, f_re, f_im = _s5_mixer(hn, i_re, i_im, prm, chained)
                if chained:
                    f_re, f_im = f_re[SUBLANES - 1:], f_im[SUBLANES - 1:]
                new_re.append(f_re.reshape(b, g, p))
                new_im.append(f_im.reshape(b, g, p))
                x = _glu_proj(gl, glu_w[j], ssm_glu_b[j], x, gate1)
            else:
                lam_init = 0.8 - 0.6 * math.exp(-0.3 * i)
                q, k, kb, v, vb = _qkv_proj(hn, w_qkv[j], attn_q_norm[j], attn_k_norm[j])
                new_k.append(k.reshape(b, t, n_heads, 2, HEAD_DIM))
                new_v.append(v.reshape(b, t, n_heads, 2 * HEAD_DIM))
                lp = attn_lambda[j].astype(F32)
                lam = jnp.exp(jnp.sum(lp[0] * lp[1])) - jnp.exp(jnp.sum(lp[2] * lp[3])) + lam_init
                if past_k is None:
                    tq, tk = _tile(t, ATTN_TQ), _tile(t, ATTN_TK)
                    k_all, v_all, past = kb, vb, 0
                else:
                    past = past_k.shape[2]
                    tq = t
                    tk = _tile(past, ATTN_TK // 2)
                    fill = jnp.zeros((b, tk - t % tk if t % tk else 0, d), BF16)
                    k_all = jnp.concatenate([past_k[j].reshape(b, past, d).astype(BF16), kb, fill], axis=1)
                    v_all = jnp.concatenate([past_v[j].reshape(b, past, d).astype(BF16), vb, fill], axis=1)
                o = _diff_attention(q, k_all, v_all, lam, attn_subln[j], past, lam_init, tq, tk)
                x = _out_proj(o, w_o[j], x, gate1)
            hn, logits = _norm_mod(x, norm_ffn[i], scale2, shift2, router_bf)
            y = _moe(hn.reshape(b * t, d), logits.reshape(b * t, -1), router_b, w1, w3, w2, i, bm)
            x = x + gate2 * y.reshape(b, t, d)
        return x, jnp.stack(new_re), jnp.stack(new_im), jnp.stack(new_k), jnp.stack(new_v)

    zero_state = jnp.zeros((state_ssm_re.shape[0], bp, g, p), F32)
    y_p, re_p, im_p, k_p, v_p = trunk(x_prompt, 0, zero_state, zero_state, None, None, True, 256)
    y_s, re_s, im_s, k_s, v_s = trunk(x_sample, bp, state_ssm_re, state_ssm_im, cache_k, cache_v, False, 128)
    return (y_p, y_s, re_p, im_p, k_p, v_p, re_s, im_s, k_s, v_s)
```

```python
import functools
import math

import numpy as np
import jax
import jax.numpy as jnp
from jax import lax
from jax.experimental import pallas as pl
from jax.experimental.pallas import tpu as pltpu

F32 = jnp.float32
BF16 = jnp.bfloat16
HIGHEST = lax.Precision.HIGHEST

NORM_EPS = 1e-6
CHUNK = 64
SSM_GROUP = 16
SSM_STATE = 64
S5_L = 16
S5_GB = 8
HEAD_DIM = 128
N_EXPERT_GROUPS = 8
TOP_K = 2
NEG_INF = -1e30
LOG2E = 1.4426950408889634
ATTN_TQ = 512
ATTN_TK = 2048
CAST_BLOCK_ELEMS = 2 * 1024 * 1024
SUBLANES = 8
VMEM_LIMIT = 52 * 1024 * 1024


def _cparams(sem):
    return pltpu.CompilerParams(dimension_semantics=sem, vmem_limit_bytes=VMEM_LIMIT)


def _tile(n, pref):
    if n <= pref:
        return n
    t = pref
    while n % t:
        t //= 2
    return t


def _cast_kernel(x_ref, o_ref):
    o_ref[...] = x_ref[...].astype(o_ref.dtype)


def _to_bf16(w):
    cols = w.shape[-1]
    rows = w.size // cols
    tr = _tile(rows, 1 << int(math.log2(CAST_BLOCK_ELEMS // cols)))
    spec = pl.BlockSpec((tr, cols), lambda i: (i, 0))
    out = pl.pallas_call(
        _cast_kernel, grid=(rows // tr,), in_specs=[spec], out_specs=spec,
        out_shape=jax.ShapeDtypeStruct((rows, cols), BF16),
        compiler_params=_cparams(("arbitrary",)), name="to_bf16",
    )(w.reshape(rows, cols))
    return out.reshape(w.shape)


def _ada_kernel(c_ref, w_ref, b_ref, o_ref):
    c = c_ref[...]
    s = c * jax.nn.sigmoid(c)
    o_ref[0] = jnp.dot(s, w_ref[0], preferred_element_type=F32, precision=HIGHEST) + b_ref[0]


def _ada_mod(c_all, ada_w, ada_b):
    depth, d, n6 = ada_w.shape
    r = c_all.shape[0]
    tn = _tile(n6, 512)
    return pl.pallas_call(
        _ada_kernel,
        grid=(depth, n6 // tn),
        in_specs=[pl.BlockSpec((r, d), lambda i, j: (0, 0)),
                  pl.BlockSpec((1, d, tn), lambda i, j: (i, 0, j)),
                  pl.BlockSpec((1, 1, tn), lambda i, j: (i, 0, j))],
        out_specs=pl.BlockSpec((1, r, tn), lambda i, j: (i, 0, j)),
        out_shape=jax.ShapeDtypeStruct((depth, r, n6), F32),
        compiler_params=_cparams(("arbitrary", "arbitrary")),
        name="ada_mod",
    )(c_all, ada_w, ada_b.reshape(depth, 1, n6))


def _normmod_kernel(x_ref, g_ref, sc_ref, sh_ref, o_ref):
    x = x_ref[0]
    y = x * lax.rsqrt(jnp.mean(x * x, axis=-1, keepdims=True) + NORM_EPS)
    o_ref[0] = (y * g_ref[...] * (1.0 + sc_ref[0]) + sh_ref[0]).astype(o_ref.dtype)


def _normmod_router_kernel(x_ref, g_ref, sc_ref, sh_ref, rw_ref, o_ref, lg_ref):
    x = x_ref[0]
    y = x * lax.rsqrt(jnp.mean(x * x, axis=-1, keepdims=True) + NORM_EPS)
    hn = (y * g_ref[...] * (1.0 + sc_ref[0]) + sh_ref[0]).astype(o_ref.dtype)
    o_ref[0] = hn
    lg_ref[0] = jnp.dot(hn, rw_ref[...], preferred_element_type=F32)


def _norm_mod(x, g, scale, shift, router_w=None, out_dtype=BF16):
    b, t, d = x.shape
    tm = _tile(t, 256)
    xspec = pl.BlockSpec((1, tm, d), lambda i, j: (i, j, 0))
    vspec = pl.BlockSpec((1, 1, d), lambda i, j: (i, 0, 0))
    gspec = pl.BlockSpec((1, d), lambda i, j: (0, 0))
    if router_w is None:
        return pl.pallas_call(
            _normmod_kernel, grid=(b, t // tm),
            in_specs=[xspec, gspec, vspec, vspec], out_specs=xspec,
            out_shape=jax.ShapeDtypeStruct((b, t, d), out_dtype),
            compiler_params=_cparams(("arbitrary", "arbitrary")), name="norm_mod",
        )(x, g.reshape(1, d), scale, shift)
    e = router_w.shape[1]
    return pl.pallas_call(
        _normmod_router_kernel, grid=(b, t // tm),
        in_specs=[xspec, gspec, vspec, vspec, pl.BlockSpec((d, e), lambda i, j: (0, 0))],
        out_specs=[xspec, pl.BlockSpec((1, tm, e), lambda i, j: (i, j, 0))],
        out_shape=[jax.ShapeDtypeStruct((b, t, d), BF16), jax.ShapeDtypeStruct((b, t, e), F32)],
        compiler_params=_cparams(("arbitrary", "arbitrary")), name="norm_mod_router",
    )(x, g.reshape(1, d), scale, shift, router_w)


def _cmul(ar, ai, br, bi):
    return ar * br - ai * bi, ar * bi + ai * br


def _cpow(re, im, n):
    out_re, out_im = jnp.ones_like(re), jnp.zeros_like(im)
    while n:
        if n & 1:
            out_re, out_im = _cmul(out_re, out_im, re, im)
        re, im = _cmul(re, im, re, im)
        n >>= 1
    return out_re, out_im


def _s5_params(a_re, a_im, log_dt, b_re, b_im, c_re, c_im, d, seg_len):
    g, p = a_re.shape
    h = SSM_GROUP
    L = S5_L
    q = g // S5_GB
    dt = jnp.exp(log_dt.astype(F32))[:, None]
    mag = jnp.exp(a_re * dt)
    abar_re = mag * jnp.cos(a_im * dt)
    abar_im = mag * jnp.sin(a_im * dt)
    den = a_re * a_re + a_im * a_im
    nr = abar_re - 1.0
    coef_re = (nr * a_re + abar_im * a_im) / den
    coef_im = (abar_im * a_re - nr * a_im) / den
    bbar_re = coef_re[..., None] * b_re - coef_im[..., None] * b_im
    bbar_im = coef_re[..., None] * b_im + coef_im[..., None] * b_re
    pw_re, pw_im = [jnp.ones_like(abar_re)], [jnp.zeros_like(abar_im)]
    for _ in range(L):
        r_, i_ = _cmul(pw_re[-1], pw_im[-1], abar_re, abar_im)
        pw_re.append(r_)
        pw_im.append(i_)
    pw_re = jnp.stack(pw_re)
    pw_im = jnp.stack(pw_im)
    w1_re = pw_re[:L, :, :, None] * bbar_re[None] - pw_im[:L, :, :, None] * bbar_im[None]
    w1_im = pw_re[:L, :, :, None] * bbar_im[None] + pw_im[:L, :, :, None] * bbar_re[None]
    k_tau = (jnp.einsum('gop,tgpi->tgio', c_re, w1_re, precision=HIGHEST)
             - jnp.einsum('gop,tgpi->tgio', c_im, w1_im, precision=HIGHEST))
    gb = S5_GB
    eye = jnp.eye(gb, dtype=F32)
    kcat = jnp.einsum('tqgio,gk->qgitko', k_tau.reshape(L, q, gb, h, h), eye).reshape(q, gb * h, L * gb * h)
    ws_re = jnp.einsum('lqgpi,gk->qlgikp', w1_re[::-1].reshape(L, q, gb, p, h), eye).reshape(q, L * gb * h, gb * p)
    ws_im = jnp.einsum('lqgpi,gk->qlgikp', w1_im[::-1].reshape(L, q, gb, p, h), eye).reshape(q, L * gb * h, gb * p)
    ws = jnp.concatenate([ws_re, ws_im], axis=-1)
    x_re = c_re[None] * pw_re[1:, :, None, :] - c_im[None] * pw_im[1:, :, None, :]
    x_im = c_re[None] * pw_im[1:, :, None, :] + c_im[None] * pw_re[1:, :, None, :]
    wy_re = jnp.einsum('lqgop,gk->qgplko', x_re.reshape(L, q, gb, h, p), eye).reshape(q, gb * p, L * gb * h)
    wy_im = jnp.einsum('lqgop,gk->qgplko', -x_im.reshape(L, q, gb, h, p), eye).reshape(q, gb * p, L * gb * h)
    wy = jnp.concatenate([wy_re, wy_im], axis=1)
    achunk_re, achunk_im = pw_re[L], pw_im[L]
    aseg_re, aseg_im = _cpow(achunk_re, achunk_im, seg_len)
    nk = gb * p // 128
    lanes = lambda a: a.reshape(q, nk, 1, 128)
    return dict(kcat=kcat.astype(BF16), ws=ws.astype(BF16), wy=wy.astype(BF16),
                ac_re=lanes(achunk_re), ac_im=lanes(achunk_im), as_re=lanes(aseg_re), as_im=lanes(aseg_im),
                d=d.reshape(q, 1, gb * h))


def _s5_state_kernel(x_ref, ws_ref, ac_re_ref, ac_im_ref, as_re_ref, as_im_ref, i_re_ref, i_im_ref,
                     hre_ref, him_ref, fre_ref, fim_ref, ucat_ref, *, seg_len, chained, nc, pb):
    for l in range(S5_L):
        ucat_ref[:, l * 128:(l + 1) * 128] = x_ref[pl.ds(l, nc, stride=S5_L), :].astype(BF16)
    s_all = jnp.dot(ucat_ref[...], ws_ref[0], preferred_element_type=F32)
    for j in range(pb):
        hre_ref[0, j] = s_all[:, j * 128:(j + 1) * 128]
        him_ref[0, j] = s_all[:, (pb + j) * 128:(pb + j + 1) * 128]

    shape = (SUBLANES, 128)
    ar = [jnp.broadcast_to(ac_re_ref[0, j], shape) for j in range(pb)]
    ai = [jnp.broadcast_to(ac_im_ref[0, j], shape) for j in range(pb)]

    def advance(i, carry):
        out = []
        for j in range(pb):
            hr, hi = carry[2 * j], carry[2 * j + 1]
            out.append(ar[j] * hr - ai[j] * hi + hre_ref[0, j, pl.ds(i, SUBLANES, stride=seg_len), :])
            out.append(ar[j] * hi + ai[j] * hr + him_ref[0, j, pl.ds(i, SUBLANES, stride=seg_len), :])
        return tuple(out)

    def store_advance(i, carry):
        new = advance(i, carry)
        for j in range(pb):
            hre_ref[0, j, pl.ds(i, SUBLANES, stride=seg_len), :] = carry[2 * j]
            him_ref[0, j, pl.ds(i, SUBLANES, stride=seg_len), :] = carry[2 * j + 1]
        return new

    start = []
    for j in range(pb):
        start += [i_re_ref[0, j], i_im_ref[0, j]]
    start = tuple(start)
    if chained:
        ends = lax.fori_loop(0, seg_len, advance, start)
        row = lax.broadcasted_iota(jnp.int32, shape, 0)
        fixed = []
        for j in range(pb):
            er, ei = ends[2 * j], ends[2 * j + 1]
            sr, si = as_re_ref[0, j], as_im_ref[0, j]
            tr, ti = er[0:1], ei[0:1]
            t_re, t_im = start[2 * j], start[2 * j + 1]
            for s in range(1, SUBLANES):
                if s > 1:
                    tr, ti = sr * tr - si * ti + er[s - 1:s], sr * ti + si * tr + ei[s - 1:s]
                t_re = jnp.where(row == s, jnp.broadcast_to(tr, shape), t_re)
                t_im = jnp.where(row == s, jnp.broadcast_to(ti, shape), t_im)
            fixed += [t_re, t_im]
        start = tuple(fixed)
    final = lax.fori_loop(0, seg_len, store_advance, start)
    for j in range(pb):
        fre_ref[0, j] = final[2 * j]
        fim_ref[0, j] = final[2 * j + 1]


def _s5_out_kernel(x_ref, kcat_ref, wy_ref, hre_ref, him_ref, d_ref, o_ref, y_ref, *, ncb, pb):
    hp = jnp.concatenate([hre_ref[0, j] for j in range(pb)] + [him_ref[0, j] for j in range(pb)],
                         axis=-1).astype(BF16)
    y_ref[...] = jnp.dot(hp, wy_ref[0], preferred_element_type=F32)
    for l_in in range(S5_L):
        u = x_ref[pl.ds(l_in, ncb, stride=S5_L), :].astype(BF16)
        width = (S5_L - l_in) * 128
        y_ref[:, l_in * 128:] += jnp.dot(u, kcat_ref[0, :, :width], preferred_element_type=F32)
    for l_out in range(S5_L):
        y = y_ref[:, l_out * 128:(l_out + 1) * 128] + d_ref[0] * x_ref[pl.ds(l_out, ncb, stride=S5_L), :]
        o_ref[pl.ds(l_out, ncb, stride=S5_L), :] = jax.nn.gelu(y, approximate=True)


def _s5_mixer(hn, h0_re, h0_im, prm, chained):
    b, t, d = hn.shape
    w = d // SSM_GROUP * SSM_STATE
    q = d // 128
    pb = S5_GB * SSM_STATE // 128
    rows = b * t
    nc = rows // S5_L
    seg_len = nc // SUBLANES
    x2 = hn.reshape(rows, d)
    lanes = lambda a: a.reshape(SUBLANES, q, pb, 128).transpose(1, 2, 0, 3)
    hblk = pl.BlockSpec((1, pb, nc, 128), lambda i: (i, 0, 0, 0))
    row = pl.BlockSpec((1, pb, 1, 128), lambda i: (i, 0, 0, 0))
    st = pl.BlockSpec((1, pb, SUBLANES, 128), lambda i: (i, 0, 0, 0))
    h_re, h_im, f_re, f_im = pl.pallas_call(
        functools.partial(_s5_state_kernel, seg_len=seg_len, chained=chained, nc=nc, pb=pb), grid=(q,),
        in_specs=[pl.BlockSpec((rows, 128), lambda i: (0, i)),
                  pl.BlockSpec((1, S5_L * 128, 2 * pb * 128), lambda i: (i, 0, 0)),
                  row, row, row, row, st, st],
        out_specs=[hblk, hblk, st, st],
        out_shape=[jax.ShapeDtypeStruct((q, pb, nc, 128), F32)] * 2
        + [jax.ShapeDtypeStruct((q, pb, SUBLANES, 128), F32)] * 2,
        scratch_shapes=[pltpu.VMEM((nc, S5_L * 128), BF16)],
        compiler_params=_cparams(("arbitrary",)), name="s5_state",
    )(x2, prm['ws'], prm['ac_re'], prm['ac_im'], prm['as_re'], prm['as_im'], lanes(h0_re), lanes(h0_im))
    ncb = _tile(nc, 256)
    hspec = pl.BlockSpec((1, pb, ncb, 128), lambda i, r: (i, 0, r, 0))
    xspec = pl.BlockSpec((ncb * S5_L, 128), lambda i, r: (r, i))
    gl = pl.pallas_call(
        functools.partial(_s5_out_kernel, ncb=ncb, pb=pb), grid=(q, nc // ncb),
        in_specs=[xspec,
                  pl.BlockSpec((1, 128, S5_L * 128), lambda i, r: (i, 0, 0)),
                  pl.BlockSpec((1, 2 * pb * 128, S5_L * 128), lambda i, r: (i, 0, 0)),
                  hspec, hspec,
                  pl.BlockSpec((1, 1, 128), lambda i, r: (i, 0, 0))],
        out_specs=xspec,
        out_shape=jax.ShapeDtypeStruct((rows, d), F32),
        scratch_shapes=[pltpu.VMEM((ncb, S5_L * 128), F32)],
        compiler_params=_cparams(("arbitrary", "arbitrary")), name="s5_out",
    )(x2, prm['kcat'], prm['wy'], h_re, h_im, prm['d'])
    unlanes = lambda a: a.transpose(2, 0, 1, 3).reshape(SUBLANES, w)
    return gl.reshape(b, t, d), unlanes(f_re), unlanes(f_im)


def _glu_kernel(g_ref, w_ref, b_ref, gt_ref, x_ref, gate_ref, o_ref, gb_ref):
    @pl.when(pl.program_id(2) == 0)
    def _():
        gb_ref[...] = g_ref[0].astype(BF16)

    acc = jnp.dot(gb_ref[...], w_ref[...], preferred_element_type=F32) + b_ref[...]
    o_ref[0] = x_ref[0] + gate_ref[0] * (gt_ref[0] * jax.nn.sigmoid(acc))


def _proj_res_kernel(a_ref, w_ref, x_ref, gate_ref, o_ref):
    acc = jnp.dot(a_ref[0], w_ref[...], preferred_element_type=F32)
    o_ref[0] = x_ref[0] + gate_ref[0] * acc


def _head_rms(acc, gain):
    outs = []
    for s in range(acc.shape[1] // HEAD_DIM):
        a = acc[:, s * HEAD_DIM:(s + 1) * HEAD_DIM]
        outs.append(a * lax.rsqrt(jnp.mean(a * a, axis=-1, keepdims=True) + NORM_EPS) * gain)
    return jnp.concatenate(outs, axis=-1) if len(outs) > 1 else outs[0]


def _q_kernel(a_ref, w_ref, n_ref, o_ref):
    acc = jnp.dot(a_ref[0], w_ref[...], preferred_element_type=F32)
    o_ref[0] = (_head_rms(acc, n_ref[...]) * (LOG2E * HEAD_DIM ** -0.5)).astype(o_ref.dtype)


def _k_kernel(a_ref, w_ref, n_ref, o_ref, ob_ref):
    acc = jnp.dot(a_ref[0], w_ref[...], preferred_element_type=F32)
    k = _head_rms(acc, n_ref[...])
    o_ref[0] = k
    ob_ref[0] = k.astype(ob_ref.dtype)


def _v_kernel(a_ref, w_ref, o_ref, ob_ref):
    acc = jnp.dot(a_ref[0], w_ref[...], preferred_element_type=F32)
    o_ref[0] = acc
    ob_ref[0] = acc.astype(ob_ref.dtype)


def _mm_specs(b, t, k, n, col0, tm_pref=1024, tn_pref=512):
    tm = _tile(t, tm_pref)
    tn = _tile(n, tn_pref)
    grid = (b, t // tm, n // tn)
    a_spec = pl.BlockSpec((1, tm, k), lambda bi, i, j: (bi, i, 0))
    w_spec = pl.BlockSpec((k, tn), lambda bi, i, j: (0, j + col0 // tn))
    o_spec = pl.BlockSpec((1, tm, tn), lambda bi, i, j: (bi, i, j))
    return tm, tn, grid, a_spec, w_spec, o_spec


_MM_SEM = ("arbitrary", "arbitrary", "arbitrary")


def _glu_proj(gl, w, bias, x, gate):
    b, t, d = gl.shape
    tm, tn, grid, a_spec, w_spec, o_spec = _mm_specs(b, t, d, d, 0, tm_pref=512)
    return pl.pallas_call(
        _glu_kernel, grid=grid,
        in_specs=[a_spec, w_spec, pl.BlockSpec((1, tn), lambda bi, i, j: (0, j)), o_spec, o_spec,
                  pl.BlockSpec((1, 1, tn), lambda bi, i, j: (bi, 0, j))],
        out_specs=o_spec, out_shape=jax.ShapeDtypeStruct((b, t, d), F32),
        scratch_shapes=[pltpu.VMEM((tm, d), BF16)],
        compiler_params=_cparams(_MM_SEM), name="glu_proj",
    )(gl, w, bias.reshape(1, d), gl, x, gate)


def _out_proj(a, w, x, gate):
    b, t, d = a.shape
    tm, tn, grid, a_spec, w_spec, o_spec = _mm_specs(b, t, d, d, 0)
    return pl.pallas_call(
        _proj_res_kernel, grid=grid,
        in_specs=[a_spec, w_spec, o_spec, pl.BlockSpec((1, 1, tn), lambda bi, i, j: (bi, 0, j))],
        out_specs=o_spec, out_shape=jax.ShapeDtypeStruct((b, t, d), F32),
        compiler_params=_cparams(_MM_SEM), name="out_proj",
    )(a, w, x, gate)


def _qkv_proj(hn, w_qkv, q_norm, k_norm):
    b, t, d = hn.shape
    nspec = pl.BlockSpec((1, HEAD_DIM), lambda bi, i, j: (0, 0))
    tm, tn, grid, a_spec, wq_spec, o_spec = _mm_specs(b, t, d, d, 0)
    q = pl.pallas_call(
        _q_kernel, grid=grid, in_specs=[a_spec, wq_spec, nspec], out_specs=o_spec,
        out_shape=jax.ShapeDtypeStruct((b, t, d), BF16),
        compiler_params=_cparams(_MM_SEM), name="q_proj",
    )(hn, w_qkv, q_norm.reshape(1, HEAD_DIM))
    wk_spec = _mm_specs(b, t, d, d, d)[4]
    k, kb = pl.pallas_call(
        _k_kernel, grid=grid, in_specs=[a_spec, wk_spec, nspec], out_specs=[o_spec, o_spec],
        out_shape=[jax.ShapeDtypeStruct((b, t, d), F32), jax.ShapeDtypeStruct((b, t, d), BF16)],
        compiler_params=_cparams(_MM_SEM), name="k_proj",
    )(hn, w_qkv, k_norm.reshape(1, HEAD_DIM))
    wv_spec = _mm_specs(b, t, d, d, 2 * d)[4]
    v, vb = pl.pallas_call(
        _v_kernel, grid=grid, in_specs=[a_spec, wv_spec], out_specs=[o_spec, o_spec],
        out_shape=[jax.ShapeDtypeStruct((b, t, d), F32), jax.ShapeDtypeStruct((b, t, d), BF16)],
        compiler_params=_cparams(_MM_SEM), name="v_proj",
    )(hn, w_qkv)
    return q, k, kb, v, vb


def _attn_kernel(qi_ref, ki_ref, last_ref, slope_ref, lam_ref, q_ref, k_ref, v_ref, sub_ref, o_ref,
                 m1_ref, l1_ref, a1_ref, m2_ref, l2_ref, a2_ref, *, tq, tk, past, out_scale):
    h = pl.program_id(1)
    step = pl.program_id(2)
    qi = qi_ref[step]
    ki = ki_ref[step]
    slope = slope_ref[h]
    q0 = past + qi * tq
    k0 = ki * tk

    @pl.when(ki == 0)
    def _():
        for m_ref, l_ref, a_ref in ((m1_ref, l1_ref, a1_ref), (m2_ref, l2_ref, a2_ref)):
            m_ref[...] = jnp.full(m_ref.shape, NEG_INF, F32)
            l_ref[...] = jnp.zeros(l_ref.shape, F32)
            a_ref[...] = jnp.zeros(a_ref.shape, F32)

    def update(bias, visible):
        v = v_ref[0]
        for idx, (m_ref, l_ref, a_ref) in enumerate(((m1_ref, l1_ref, a1_ref), (m2_ref, l2_ref, a2_ref))):
            qm = q_ref[0, :, idx * HEAD_DIM:(idx + 1) * HEAD_DIM]
            km = k_ref[0, :, idx * HEAD_DIM:(idx + 1) * HEAD_DIM]
            s = lax.dot_general(qm, km, (((1,), (1,)), ((), ())), preferred_element_type=F32) + bias
            if visible is not None:
                s = jnp.where(visible, s, NEG_INF)
            m_old = m_ref[...]
            m_new = jnp.maximum(m_old, jnp.max(s, axis=-1, keepdims=True))
            alpha = jnp.exp2(m_old - m_new)
            p = jnp.exp2(s - m_new)
            l_ref[...] = alpha * l_ref[...] + jnp.sum(p, axis=-1, keepdims=True)
            a_ref[...] = alpha * a_ref[...] + jnp.dot(p.astype(v.dtype), v, preferred_element_type=F32)
            m_ref[...] = m_new

    strictly_past = (k0 + tk - 1) <= q0

    @pl.when(strictly_past)
    def _():
        kpos = k0 + lax.broadcasted_iota(jnp.int32, (1, tk), 1)
        update(slope * (kpos - q0).astype(F32), None)

    @pl.when(jnp.logical_not(strictly_past))
    def _():
        qpos = q0 + lax.broadcasted_iota(jnp.int32, (tq, tk), 0)
        kpos = k0 + lax.broadcasted_iota(jnp.int32, (tq, tk), 1)
        bias = slope * ((qpos - q0) - jnp.abs(qpos - kpos)).astype(F32)
        update(bias, (kpos // CHUNK) <= (qpos // CHUNK))

    @pl.when(last_ref[step] == 1)
    def _():
        o = a1_ref[...] / l1_ref[...] - lam_ref[0] * (a2_ref[...] / l2_ref[...])
        o = o * lax.rsqrt(jnp.mean(o * o, axis=-1, keepdims=True) + NORM_EPS) * sub_ref[...]
        o_ref[0] = (o * out_scale).astype(o_ref.dtype)


def _attn_tables(tq_len, tk_len, past, tq, tk):
    qi_l, ki_l, last_l = [], [], []
    for qi in range(tq_len // tq):
        q_last_chunk = (past + qi * tq + tq - 1) // CHUNK
        ks = [ki for ki in range(tk_len // tk) if (ki * tk) // CHUNK <= q_last_chunk]
        for ki in ks:
            qi_l.append(qi)
            ki_l.append(ki)
            last_l.append(1 if ki == ks[-1] else 0)
    return (np.asarray(qi_l, np.int32), np.asarray(ki_l, np.int32), np.asarray(last_l, np.int32))


def _diff_attention(q, k, v, lam, subln, past, lam_init, tq, tk):
    b, tq_len, d = q.shape
    tk_len = k.shape[1]
    n_heads = d // (2 * HEAD_DIM)
    hw = 2 * HEAD_DIM
    qi_t, ki_t, last_t = _attn_tables(tq_len, tk_len, past, tq, tk)
    slopes = jnp.asarray(LOG2E * 2.0 ** (-8.0 * np.arange(1, n_heads + 1, dtype=np.float32) / n_heads), F32)
    grid_spec = pltpu.PrefetchScalarGridSpec(
        num_scalar_prefetch=5,
        grid=(b, n_heads, len(qi_t)),
        in_specs=[pl.BlockSpec((1, tq, hw), lambda bi, h, s, qi, ki, *_: (bi, qi[s], h)),
                  pl.BlockSpec((1, tk, hw), lambda bi, h, s, qi, ki, *_: (bi, ki[s], h)),
                  pl.BlockSpec((1, tk, hw), lambda bi, h, s, qi, ki, *_: (bi, ki[s], h)),
                  pl.BlockSpec((1, hw), lambda bi, h, s, *_: (0, 0))],
        out_specs=pl.BlockSpec((1, tq, hw), lambda bi, h, s, qi, ki, *_: (bi, qi[s], h)),
        scratch_shapes=[pltpu.VMEM((tq, 1), F32), pltpu.VMEM((tq, 1), F32), pltpu.VMEM((tq, hw), F32),
                        pltpu.VMEM((tq, 1), F32), pltpu.VMEM((tq, 1), F32), pltpu.VMEM((tq, hw), F32)],
    )
    return pl.pallas_call(
        functools.partial(_attn_kernel, tq=tq, tk=tk, past=past, out_scale=1.0 - lam_init),
        grid_spec=grid_spec,
        out_shape=jax.ShapeDtypeStruct((b, tq_len, d), BF16),
        compiler_params=_cparams(("arbitrary", "arbitrary", "arbitrary")), name="diff_attn",
    )(jnp.asarray(qi_t), jnp.asarray(ki_t), jnp.asarray(last_t), slopes, lam.reshape(1).astype(F32),
      q, k, v, subln.reshape(1, hw).astype(F32))


def _moe_kernel(be_ref, nu_ref, xs_ref, w1_ref, w3_ref, w2_ref, o_ref):
    blk = pl.program_id(0)

    @pl.when(blk < nu_ref[0])
    def _():
        x = xs_ref[...]
        h1 = jnp.dot(x, w1_ref[0, 0], preferred_element_type=F32)
        h3 = jnp.dot(x, w3_ref[0, 0], preferred_element_type=F32)
        hid = (h1 * jax.nn.sigmoid(h1)) * h3
        o_ref[...] = jnp.dot(hid.astype(BF16), w2_ref[0, 0], preferred_element_type=F32)

    @pl.when(blk >= nu_ref[0])
    def _():
        o_ref[...] = jnp.zeros(o_ref.shape, o_ref.dtype)


def _top2(x):
    col = lax.broadcasted_iota(jnp.int32, x.shape, x.ndim - 1)
    i0 = jnp.argmax(x, axis=-1).astype(jnp.int32)
    v0 = jnp.max(x, axis=-1)
    rest = jnp.where(col == i0[..., None], -jnp.inf, x)
    i1 = jnp.argmax(rest, axis=-1).astype(jnp.int32)
    v1 = jnp.max(rest, axis=-1)
    return (v0, v1), (i0, i1)


def _route(logits, router_b):
    n, e = logits.shape
    per = e // N_EXPERT_GROUPS
    scores = jax.nn.sigmoid(logits)
    sel_g = (scores + router_b.astype(F32)).reshape(n, N_EXPERT_GROUPS, per)
    (v0, v1), _ = _top2(sel_g)
    g_idx = jnp.argmax(v0 + v1, axis=-1).astype(jnp.int32)
    in_group = jnp.take_along_axis(sel_g, g_idx[:, None, None], axis=1)[:, 0]
    _, (l0, l1) = _top2(in_group)
    expert_idx = g_idx[:, None] * per + jnp.stack([l0, l1], axis=-1)
    w = jnp.take_along_axis(scores, expert_idx, axis=1)
    return expert_idx, w / jnp.sum(w, axis=-1, keepdims=True)


def _cumsum_rows(mask):
    n, e = mask.shape
    rb = _tile(n, 128)
    m3 = mask.reshape(n // rb, rb, e).astype(BF16)
    tri = jnp.tril(jnp.ones((rb, rb), BF16))
    within = jnp.einsum('ij,bje->bie', tri, m3, preferred_element_type=F32).astype(jnp.int32)
    totals = within[:, -1, :]
    offs = jnp.cumsum(totals, axis=0) - totals
    return (within + offs[:, None, :]).reshape(n, e)


def _moe(hn, logits, router_b, w1, w3, w2, layer, bm):
    n, d = hn.shape
    e, f = w1.shape[1], w1.shape[3]
    expert_idx, gate = _route(logits, router_b)
    n_assign = n * TOP_K
    flat_e = expert_idx.reshape(n_assign)
    onehot = flat_e[:, None] == jnp.arange(e, dtype=jnp.int32)[None, :]
    csum = _cumsum_rows(onehot)
    counts = csum[-1]
    rank = jnp.take_along_axis(csum, flat_e[:, None], axis=1)[:, 0] - 1
    padded = (counts + bm - 1) // bm * bm
    pad_end = jnp.cumsum(padded)
    pad_start = pad_end - padded
    dest = pad_start[flat_e] + rank
    n_blocks = -(-n_assign // bm) + e
    slot_tok = jnp.zeros((n_blocks * bm,), jnp.int32).at[dest].set(
        jnp.arange(n_assign, dtype=jnp.int32) // TOP_K)
    xs = hn[slot_tok]
    block_e = jnp.minimum(
        jnp.searchsorted(pad_end, jnp.arange(n_blocks, dtype=jnp.int32) * bm, side='right'),
        e - 1).astype(jnp.int32)
    n_used = (pad_end[-1] // bm).astype(jnp.int32).reshape(1)
    once = pl.Buffered(1)
    grid_spec = pltpu.PrefetchScalarGridSpec(
        num_scalar_prefetch=2,
        grid=(n_blocks,),
        in_specs=[pl.BlockSpec((bm, d), lambda bi, be, nu: (bi, 0)),
                  pl.BlockSpec((1, 1, d, f), lambda bi, be, nu: (layer, be[bi], 0, 0), pipeline_mode=once),
                  pl.BlockSpec((1, 1, d, f), lambda bi, be, nu: (layer, be[bi], 0, 0), pipeline_mode=once),
                  pl.BlockSpec((1, 1, f, d), lambda bi, be, nu: (layer, be[bi], 0, 0), pipeline_mode=once)],
        out_specs=pl.BlockSpec((bm, d), lambda bi, be, nu: (bi, 0)),
    )
    ys = pl.pallas_call(
        _moe_kernel, grid_spec=grid_spec,
        out_shape=jax.ShapeDtypeStruct((n_blocks * bm, d), F32),
        compiler_params=_cparams(("arbitrary",)), name="moe_experts",
    )(block_e, n_used, xs, w1, w3, w2)
    dest = dest.reshape(n, TOP_K)
    return gate[:, 0:1] * ys[dest[:, 0]] + gate[:, 1:2] * ys[dest[:, 1]]


def kernel(x_prompt, x_sample, c_prompt, c_sample, state_ssm_re, state_ssm_im, cache_k, cache_v, ada_w, ada_b, norm_mix, norm_ffn, ssm_a_re, ssm_a_im, ssm_log_dt, ssm_b_re, ssm_b_im, ssm_c_re, ssm_c_im, ssm_d, ssm_glu_w, ssm_glu_b, attn_w_qkv, attn_w_o, attn_q_norm, attn_k_norm, attn_lambda, attn_subln, router_w, router_b, moe_w1, moe_w3, moe_w2):
    depth, d = norm_mix.shape
    n_mixers = 2
    bp, bs = x_prompt.shape[0], x_sample.shape[0]
    assert bp == 1 and bs == SUBLANES, "the S5 scan maps one long sequence or eight sequences onto sublanes"
    n_heads = d // (2 * HEAD_DIM)
    g, p = ssm_a_re.shape[1:]
    past_len = cache_k.shape[2]

    r_all = -(-(bp + bs) // SUBLANES) * SUBLANES
    c_all = jnp.concatenate([c_prompt, c_sample, jnp.zeros((r_all - bp - bs, d), F32)])
    mod_all = _ada_mod(c_all, ada_w, ada_b).reshape(depth, r_all, 6, d)

    glu_w = _to_bf16(ssm_glu_w)
    w_qkv = _to_bf16(attn_w_qkv)
    w_o = _to_bf16(attn_w_o)
    w1, w3, w2 = _to_bf16(moe_w1), _to_bf16(moe_w3), _to_bf16(moe_w2)
    router_bf = router_w.astype(BF16)

    def trunk(x, row0, h0_re, h0_im, past_k, past_v, chained, bm):
        b, t, _ = x.shape
        new_re, new_im, new_k, new_v = [], [], [], []
        for i in range(depth):
            mod = mod_all[i, row0:row0 + b][:, :, None, :]
            shift1, scale1, gate1, shift2, scale2, gate2 = (mod[:, m] for m in range(6))
            j = i // n_mixers
            hn = _norm_mod(x, norm_mix[i], scale1, shift1, out_dtype=F32 if i % n_mixers == 0 else BF16)
            if i % n_mixers == 0:
                seg_len = b * (t // S5_L) // SUBLANES
                prm = _s5_params(ssm_a_re[j], ssm_a_im[j], ssm_log_dt[j], ssm_b_re[j], ssm_b_im[j],
                                 ssm_c_re[j], ssm_c_im[j], ssm_d[j], seg_len)
                if chained:
                    pad = jnp.zeros((SUBLANES - 1, g * p), F32)
                    i_re = jnp.concatenate([h0_re[j].reshape(1, g * p), pad])
                    i_im = jnp.concatenate([h0_im[j].reshape(1, g * p), pad])
                else:
                    i_re, i_im = h0_re[j].reshape(b, g * p), h0_im[j].reshape(b, g * p)
                gl, f_re, f_im = _s5_mixer(hn, i_re, i_im, prm, chained)
                if chained:
                    f_re, f_im = f_re[SUBLANES - 1:], f_im[SUBLANES - 1:]
                new_re.append(f_re.reshape(b, g, p))
                new_im.append(f_im.reshape(b, g, p))
                x = _glu_proj(gl, glu_w[j], ssm_glu_b[j], x, gate1)
            else:
                lam_init = 0.8 - 0.6 * math.exp(-0.3 * i)
                q, k, kb, v, vb = _qkv_proj(hn, w_qkv[j], attn_q_norm[j], attn_k_norm[j])
                new_k.append(k.reshape(b, t, n_heads, 2, HEAD_DIM))
                new_v.append(v.reshape(b, t, n_heads, 2 * HEAD_DIM))
                lp = attn_lambda[j].astype(F32)
                lam = jnp.exp(jnp.sum(lp[0] * lp[1])) - jnp.exp(jnp.sum(lp[2] * lp[3])) + lam_init
                if past_k is None:
                    tq, tk = _tile(t, ATTN_TQ), _tile(t, ATTN_TK)
                    k_all, v_all, past = kb, vb, 0
                else:
                    past = past_k.shape[2]
                    tq = t
                    tk = _tile(past, ATTN_TK // 2)
                    fill = jnp.zeros((b, tk - t % tk if t % tk else 0, d), BF16)
                    k_all = jnp.concatenate([past_k[j].reshape(b, past, d).astype(BF16), kb, fill], axis=1)
                    v_all = jnp.concatenate([past_v[j].reshape(b, past, d).astype(BF16), vb, fill], axis=1)
                o = _diff_attention(q, k_all, v_all, lam, attn_subln[j], past, lam_init, tq, tk)
                x = _out_proj(o, w_o[j], x, gate1)
            hn, logits = _norm_mod(x, norm_ffn[i], scale2, shift2, router_bf)
            y = _moe(hn.reshape(b * t, d), logits.reshape(b * t, -1), router_b, w1, w3, w2, i, bm)
            x = x + gate2 * y.reshape(b, t, d)
        return x, jnp.stack(new_re), jnp.stack(new_im), jnp.stack(new_k), jnp.stack(new_v)

    zero_state = jnp.zeros((state_ssm_re.shape[0], bp, g, p), F32)
    y_p, re_p, im_p, k_p, v_p = trunk(x_prompt, 0, zero_state, zero_state, None, None, True, 256)
    y_s, re_s, im_s, k_s, v_s = trunk(x_sample, bp, state_ssm_re, state_ssm_im, cache_k, cache_v, False, 128)
    return (y_p, y_s, re_p, im_p, k_p, v_p, re_s, im_s, k_s, v_s)
```

```python
import functools
import math

import numpy as np
import jax
import jax.numpy as jnp
from jax import lax
from jax.experimental import pallas as pl
from jax.experimental.pallas import tpu as pltpu

F32 = jnp.float32
BF16 = jnp.bfloat16
HIGHEST = lax.Precision.HIGHEST

NORM_EPS = 1e-6
CHUNK = 64
SSM_GROUP = 16
SSM_STATE = 64
S5_L = 16
S5_GB = 8
HEAD_DIM = 128
N_EXPERT_GROUPS = 8
TOP_K = 2
NEG_INF = -1e30
LOG2E = 1.4426950408889634
ATTN_TQ = 512
ATTN_TK = 2048
CAST_BLOCK_ELEMS = 2 * 1024 * 1024
SUBLANES = 8
VMEM_LIMIT = 52 * 1024 * 1024


def _cparams(sem):
    return pltpu.CompilerParams(dimension_semantics=sem, vmem_limit_bytes=VMEM_LIMIT)


def _tile(n, pref):
    if n <= pref:
        return n
    t = pref
    while n % t:
        t //= 2
    return t


def _cast_kernel(x_ref, o_ref):
    o_ref[...] = x_ref[...].astype(o_ref.dtype)


def _to_bf16(w):
    cols = w.shape[-1]
    rows = w.size // cols
    tr = _tile(rows, 1 << int(math.log2(CAST_BLOCK_ELEMS // cols)))
    spec = pl.BlockSpec((tr, cols), lambda i: (i, 0))
    out = pl.pallas_call(
        _cast_kernel, grid=(rows // tr,), in_specs=[spec], out_specs=spec,
        out_shape=jax.ShapeDtypeStruct((rows, cols), BF16),
        compiler_params=_cparams(("arbitrary",)), name="to_bf16",
    )(w.reshape(rows, cols))
    return out.reshape(w.shape)


def _ada_kernel(c_ref, w_ref, b_ref, o_ref):
    c = c_ref[...]
    s = (c * jax.nn.sigmoid(c)).astype(BF16)
    o_ref[0] = jnp.dot(s, w_ref[0].astype(BF16), preferred_element_type=F32) + b_ref[0]


def _ada_mod(c_all, ada_w, ada_b):
    depth, d, n6 = ada_w.shape
    r = c_all.shape[0]
    tn = _tile(n6, 512)
    return pl.pallas_call(
        _ada_kernel,
        grid=(depth, n6 // tn),
        in_specs=[pl.BlockSpec((r, d), lambda i, j: (0, 0)),
                  pl.BlockSpec((1, d, tn), lambda i, j: (i, 0, j)),
                  pl.BlockSpec((1, 1, tn), lambda i, j: (i, 0, j))],
        out_specs=pl.BlockSpec((1, r, tn), lambda i, j: (i, 0, j)),
        out_shape=jax.ShapeDtypeStruct((depth, r, n6), F32),
        compiler_params=_cparams(("arbitrary", "arbitrary")),
        name="ada_mod",
    )(c_all, ada_w, ada_b.reshape(depth, 1, n6))


def _normmod_kernel(x_ref, g_ref, sc_ref, sh_ref, o_ref):
    x = x_ref[0]
    y = x * lax.rsqrt(jnp.mean(x * x, axis=-1, keepdims=True) + NORM_EPS)
    o_ref[0] = (y * g_ref[...] * (1.0 + sc_ref[0]) + sh_ref[0]).astype(o_ref.dtype)


def _normmod_router_kernel(x_ref, g_ref, sc_ref, sh_ref, rw_ref, o_ref, lg_ref):
    x = x_ref[0]
    y = x * lax.rsqrt(jnp.mean(x * x, axis=-1, keepdims=True) + NORM_EPS)
    hn = (y * g_ref[...] * (1.0 + sc_ref[0]) + sh_ref[0]).astype(o_ref.dtype)
    o_ref[0] = hn
    lg_ref[0] = jnp.dot(hn, rw_ref[...], preferred_element_type=F32)


def _norm_mod(x, g, scale, shift, router_w=None, out_dtype=BF16):
    b, t, d = x.shape
    tm = _tile(t, 256)
    xspec = pl.BlockSpec((1, tm, d), lambda i, j: (i, j, 0))
    vspec = pl.BlockSpec((1, 1, d), lambda i, j: (i, 0, 0))
    gspec = pl.BlockSpec((1, d), lambda i, j: (0, 0))
    if router_w is None:
        return pl.pallas_call(
            _normmod_kernel, grid=(b, t // tm),
            in_specs=[xspec, gspec, vspec, vspec], out_specs=xspec,
            out_shape=jax.ShapeDtypeStruct((b, t, d), out_dtype),
            compiler_params=_cparams(("arbitrary", "arbitrary")), name="norm_mod",
        )(x, g.reshape(1, d), scale, shift)
    e = router_w.shape[1]
    return pl.pallas_call(
        _normmod_router_kernel, grid=(b, t // tm),
        in_specs=[xspec, gspec, vspec, vspec, pl.BlockSpec((d, e), lambda i, j: (0, 0))],
        out_specs=[xspec, pl.BlockSpec((1, tm, e), lambda i, j: (i, j, 0))],
        out_shape=[jax.ShapeDtypeStruct((b, t, d), BF16), jax.ShapeDtypeStruct((b, t, e), F32)],
        compiler_params=_cparams(("arbitrary", "arbitrary")), name="norm_mod_router",
    )(x, g.reshape(1, d), scale, shift, router_w)


def _cmul(ar, ai, br, bi):
    return ar * br - ai * bi, ar * bi + ai * br


def _cpow(re, im, n):
    out_re, out_im = jnp.ones_like(re), jnp.zeros_like(im)
    while n:
        if n & 1:
            out_re, out_im = _cmul(out_re, out_im, re, im)
        re, im = _cmul(re, im, re, im)
        n >>= 1
    return out_re, out_im


def _s5_params(a_re, a_im, log_dt, b_re, b_im, c_re, c_im, d, seg_len):
    g, p = a_re.shape
    h = SSM_GROUP
    L = S5_L
    q = g // S5_GB
    dt = jnp.exp(log_dt.astype(F32))[:, None]
    mag = jnp.exp(a_re * dt)
    abar_re = mag * jnp.cos(a_im * dt)
    abar_im = mag * jnp.sin(a_im * dt)
    den = a_re * a_re + a_im * a_im
    nr = abar_re - 1.0
    coef_re = (nr * a_re + abar_im * a_im) / den
    coef_im = (abar_im * a_re - nr * a_im) / den
    bbar_re = coef_re[..., None] * b_re - coef_im[..., None] * b_im
    bbar_im = coef_re[..., None] * b_im + coef_im[..., None] * b_re
    pw_re, pw_im = [jnp.ones_like(abar_re)], [jnp.zeros_like(abar_im)]
    for _ in range(L):
        r_, i_ = _cmul(pw_re[-1], pw_im[-1], abar_re, abar_im)
        pw_re.append(r_)
        pw_im.append(i_)
    pw_re = jnp.stack(pw_re)
    pw_im = jnp.stack(pw_im)
    w1_re = pw_re[:L, :, :, None] * bbar_re[None] - pw_im[:L, :, :, None] * bbar_im[None]
    w1_im = pw_re[:L, :, :, None] * bbar_im[None] + pw_im[:L, :, :, None] * bbar_re[None]
    k_tau = (jnp.einsum('gop,tgpi->tgio', c_re, w1_re, precision=HIGHEST)
             - jnp.einsum('gop,tgpi->tgio', c_im, w1_im, precision=HIGHEST))
    gb = S5_GB
    eye = jnp.eye(gb, dtype=F32)
    kcat = jnp.einsum('tqgio,gk->qgitko', k_tau.reshape(L, q, gb, h, h), eye).reshape(q, gb * h, L * gb * h)
    ws_re = jnp.einsum('lqgpi,gk->qlgikp', w1_re[::-1].reshape(L, q, gb, p, h), eye).reshape(q, L * gb * h, gb * p)
    ws_im = jnp.einsum('lqgpi,gk->qlgikp', w1_im[::-1].reshape(L, q, gb, p, h), eye).reshape(q, L * gb * h, gb * p)
    ws = jnp.concatenate([ws_re, ws_im], axis=-1)
    x_re = c_re[None] * pw_re[1:, :, None, :] - c_im[None] * pw_im[1:, :, None, :]
    x_im = c_re[None] * pw_im[1:, :, None, :] + c_im[None] * pw_re[1:, :, None, :]
    wy_re = jnp.einsum('lqgop,gk->qgplko', x_re.reshape(L, q, gb, h, p), eye).reshape(q, gb * p, L * gb * h)
    wy_im = jnp.einsum('lqgop,gk->qgplko', -x_im.reshape(L, q, gb, h, p), eye).reshape(q, gb * p, L * gb * h)
    wy = jnp.concatenate([wy_re, wy_im], axis=1)
    achunk_re, achunk_im = pw_re[L], pw_im[L]
    aseg_re, aseg_im = _cpow(achunk_re, achunk_im, seg_len)
    nk = gb * p // 128
    lanes = lambda a: a.reshape(q, nk, 1, 128)
    return dict(kcat=kcat.astype(BF16), ws=ws.astype(BF16), wy=wy.astype(BF16),
                ac_re=lanes(achunk_re), ac_im=lanes(achunk_im), as_re=lanes(aseg_re), as_im=lanes(aseg_im),
                d=d.reshape(q, 1, gb * h))


def _s5_state_kernel(x_ref, ws_ref, ac_re_ref, ac_im_ref, as_re_ref, as_im_ref, i_re_ref, i_im_ref,
                     hre_ref, him_ref, fre_ref, fim_ref, ucat_ref, *, seg_len, chained, nc, pb):
    for l in range(S5_L):
        ucat_ref[:, l * 128:(l + 1) * 128] = x_ref[pl.ds(l, nc, stride=S5_L), :].astype(BF16)
    s_all = jnp.dot(ucat_ref[...], ws_ref[0], preferred_element_type=F32)
    for j in range(pb):
        hre_ref[0, j] = s_all[:, j * 128:(j + 1) * 128]
        him_ref[0, j] = s_all[:, (pb + j) * 128:(pb + j + 1) * 128]

    shape = (SUBLANES, 128)
    ar = [jnp.broadcast_to(ac_re_ref[0, j], shape) for j in range(pb)]
    ai = [jnp.broadcast_to(ac_im_ref[0, j], shape) for j in range(pb)]

    def advance(i, carry):
        out = []
        for j in range(pb):
            hr, hi = carry[2 * j], carry[2 * j + 1]
            out.append(ar[j] * hr - ai[j] * hi + hre_ref[0, j, pl.ds(i, SUBLANES, stride=seg_len), :])
            out.append(ar[j] * hi + ai[j] * hr + him_ref[0, j, pl.ds(i, SUBLANES, stride=seg_len), :])
        return tuple(out)

    def store_advance(i, carry):
        new = advance(i, carry)
        for j in range(pb):
            hre_ref[0, j, pl.ds(i, SUBLANES, stride=seg_len), :] = carry[2 * j]
            him_ref[0, j, pl.ds(i, SUBLANES, stride=seg_len), :] = carry[2 * j + 1]
        return new

    start = []
    for j in range(pb):
        start += [i_re_ref[0, j], i_im_ref[0, j]]
    start = tuple(start)
    if chained:
        ends = lax.fori_loop(0, seg_len, advance, start)
        row = lax.broadcasted_iota(jnp.int32, shape, 0)
        fixed = []
        for j in range(pb):
            er, ei = ends[2 * j], ends[2 * j + 1]
            sr, si = as_re_ref[0, j], as_im_ref[0, j]
            tr, ti = er[0:1], ei[0:1]
            t_re, t_im = start[2 * j], start[2 * j + 1]
            for s in range(1, SUBLANES):
                if s > 1:
                    tr, ti = sr * tr - si * ti + er[s - 1:s], sr * ti + si * tr + ei[s - 1:s]
                t_re = jnp.where(row == s, jnp.broadcast_to(tr, shape), t_re)
                t_im = jnp.where(row == s, jnp.broadcast_to(ti, shape), t_im)
            fixed += [t_re, t_im]
        start = tuple(fixed)
    final = lax.fori_loop(0, seg_len, store_advance, start)
    for j in range(pb):
        fre_ref[0, j] = final[2 * j]
        fim_ref[0, j] = final[2 * j + 1]


def _s5_out_kernel(x_ref, kcat_ref, wy_ref, hre_ref, him_ref, d_ref, o_ref, y_ref, *, ncb, pb):
    hp = jnp.concatenate([hre_ref[0, j] for j in range(pb)] + [him_ref[0, j] for j in range(pb)],
                         axis=-1).astype(BF16)
    y_ref[...] = jnp.dot(hp, wy_ref[0], preferred_element_type=F32)
    for l_in in range(S5_L):
        u = x_ref[pl.ds(l_in, ncb, stride=S5_L), :].astype(BF16)
        width = (S5_L - l_in) * 128
        y_ref[:, l_in * 128:] += jnp.dot(u, kcat_ref[0, :, :width], preferred_element_type=F32)
    for l_out in range(S5_L):
        y = y_ref[:, l_out * 128:(l_out + 1) * 128] + d_ref[0] * x_ref[pl.ds(l_out, ncb, stride=S5_L), :]
        o_ref[pl.ds(l_out, ncb, stride=S5_L), :] = jax.nn.gelu(y, approximate=True)


def _s5_mixer(hn, h0_re, h0_im, prm, chained):
    b, t, d = hn.shape
    w = d // SSM_GROUP * SSM_STATE
    q = d // 128
    pb = S5_GB * SSM_STATE // 128
    rows = b * t
    nc = rows // S5_L
    seg_len = nc // SUBLANES
    x2 = hn.reshape(rows, d)
    lanes = lambda a: a.reshape(SUBLANES, q, pb, 128).transpose(1, 2, 0, 3)
    hblk = pl.BlockSpec((1, pb, nc, 128), lambda i: (i, 0, 0, 0))
    row = pl.BlockSpec((1, pb, 1, 128), lambda i: (i, 0, 0, 0))
    st = pl.BlockSpec((1, pb, SUBLANES, 128), lambda i: (i, 0, 0, 0))
    h_re, h_im, f_re, f_im = pl.pallas_call(
        functools.partial(_s5_state_kernel, seg_len=seg_len, chained=chained, nc=nc, pb=pb), grid=(q,),
        in_specs=[pl.BlockSpec((rows, 128), lambda i: (0, i)),
                  pl.BlockSpec((1, S5_L * 128, 2 * pb * 128), lambda i: (i, 0, 0)),
                  row, row, row, row, st, st],
        out_specs=[hblk, hblk, st, st],
        out_shape=[jax.ShapeDtypeStruct((q, pb, nc, 128), F32)] * 2
        + [jax.ShapeDtypeStruct((q, pb, SUBLANES, 128), F32)] * 2,
        scratch_shapes=[pltpu.VMEM((nc, S5_L * 128), BF16)],
        compiler_params=_cparams(("arbitrary",)), name="s5_state",
    )(x2, prm['ws'], prm['ac_re'], prm['ac_im'], prm['as_re'], prm['as_im'], lanes(h0_re), lanes(h0_im))
    ncb = _tile(nc, 256)
    hspec = pl.BlockSpec((1, pb, ncb, 128), lambda i, r: (i, 0, r, 0))
    xspec = pl.BlockSpec((ncb * S5_L, 128), lambda i, r: (r, i))
    gl = pl.pallas_call(
        functools.partial(_s5_out_kernel, ncb=ncb, pb=pb), grid=(q, nc // ncb),
        in_specs=[xspec,
                  pl.BlockSpec((1, 128, S5_L * 128), lambda i, r: (i, 0, 0)),
                  pl.BlockSpec((1, 2 * pb * 128, S5_L * 128), lambda i, r: (i, 0, 0)),
                  hspec, hspec,
                  pl.BlockSpec((1, 1, 128), lambda i, r: (i, 0, 0))],
        out_specs=xspec,
        out_shape=jax.ShapeDtypeStruct((rows, d), F32),
        scratch_shapes=[pltpu.VMEM((ncb, S5_L * 128), F32)],
        compiler_params=_cparams(("arbitrary", "arbitrary")), name="s5_out",
    )(x2, prm['kcat'], prm['wy'], h_re, h_im, prm['d'])
    unlanes = lambda a: a.transpose(2, 0, 1, 3).reshape(SUBLANES, w)
    return gl.reshape(b, t, d), unlanes(f_re), unlanes(f_im)


def _glu_kernel(g_ref, w_ref, b_ref, gt_ref, x_ref, gate_ref, o_ref, gb_ref):
    @pl.when(pl.program_id(2) == 0)
    def _():
        gb_ref[...] = g_ref[0].astype(BF16)

    acc = jnp.dot(gb_ref[...], w_ref[...], preferred_element_type=F32) + b_ref[...]
    o_ref[0] = x_ref[0] + gate_ref[0] * (gt_ref[0] * jax.nn.sigmoid(acc))


def _proj_res_kernel(a_ref, w_ref, x_ref, gate_ref, o_ref):
    acc = jnp.dot(a_ref[0], w_ref[...], preferred_element_type=F32)
    o_ref[0] = x_ref[0] + gate_ref[0] * acc


def _head_rms(acc, gain):
    outs = []
    for s in range(acc.shape[1] // HEAD_DIM):
        a = acc[:, s * HEAD_DIM:(s + 1) * HEAD_DIM]
        outs.append(a * lax.rsqrt(jnp.mean(a * a, axis=-1, keepdims=True) + NORM_EPS) * gain)
    return jnp.concatenate(outs, axis=-1) if len(outs) > 1 else outs[0]


def _q_kernel(a_ref, w_ref, n_ref, o_ref):
    acc = jnp.dot(a_ref[0], w_ref[...], preferred_element_type=F32)
    o_ref[0] = (_head_rms(acc, n_ref[...]) * (LOG2E * HEAD_DIM ** -0.5)).astype(o_ref.dtype)


def _k_kernel(a_ref, w_ref, n_ref, o_ref, ob_ref):
    acc = jnp.dot(a_ref[0], w_ref[...], preferred_element_type=F32)
    k = _head_rms(acc, n_ref[...])
    o_ref[0] = k
    ob_ref[0] = k.astype(ob_ref.dtype)


def _v_kernel(a_ref, w_ref, o_ref, ob_ref, *, transposed):
    acc = jnp.dot(a_ref[0], w_ref[...], preferred_element_type=F32)
    o_ref[0] = acc
    ob_ref[0] = (acc.T if transposed else acc).astype(ob_ref.dtype)


def _mm_specs(b, t, k, n, col0, tm_pref=1024, tn_pref=512):
    tm = _tile(t, tm_pref)
    tn = _tile(n, tn_pref)
    grid = (b, t // tm, n // tn)
    a_spec = pl.BlockSpec((1, tm, k), lambda bi, i, j: (bi, i, 0))
    w_spec = pl.BlockSpec((k, tn), lambda bi, i, j: (0, j + col0 // tn))
    o_spec = pl.BlockSpec((1, tm, tn), lambda bi, i, j: (bi, i, j))
    return tm, tn, grid, a_spec, w_spec, o_spec


_MM_SEM = ("arbitrary", "arbitrary", "arbitrary")


def _glu_proj(gl, w, bias, x, gate):
    b, t, d = gl.shape
    tm, tn, grid, a_spec, w_spec, o_spec = _mm_specs(b, t, d, d, 0, tm_pref=512)
    return pl.pallas_call(
        _glu_kernel, grid=grid,
        in_specs=[a_spec, w_spec, pl.BlockSpec((1, tn), lambda bi, i, j: (0, j)), o_spec, o_spec,
                  pl.BlockSpec((1, 1, tn), lambda bi, i, j: (bi, 0, j))],
        out_specs=o_spec, out_shape=jax.ShapeDtypeStruct((b, t, d), F32),
        scratch_shapes=[pltpu.VMEM((tm, d), BF16)],
        compiler_params=_cparams(_MM_SEM), name="glu_proj",
    )(gl, w, bias.reshape(1, d), gl, x, gate)


def _out_proj(a, w, x, gate):
    b, t, d = a.shape
    tm, tn, grid, a_spec, w_spec, o_spec = _mm_specs(b, t, d, d, 0)
    return pl.pallas_call(
        _proj_res_kernel, grid=grid,
        in_specs=[a_spec, w_spec, o_spec, pl.BlockSpec((1, 1, tn), lambda bi, i, j: (bi, 0, j))],
        out_specs=o_spec, out_shape=jax.ShapeDtypeStruct((b, t, d), F32),
        compiler_params=_cparams(_MM_SEM), name="out_proj",
    )(a, w, x, gate)


def _qkv_proj(hn, w_qkv, q_norm, k_norm, v_transposed):
    b, t, d = hn.shape
    nspec = pl.BlockSpec((1, HEAD_DIM), lambda bi, i, j: (0, 0))
    tm, tn, grid, a_spec, wq_spec, o_spec = _mm_specs(b, t, d, d, 0)
    q = pl.pallas_call(
        _q_kernel, grid=grid, in_specs=[a_spec, wq_spec, nspec], out_specs=o_spec,
        out_shape=jax.ShapeDtypeStruct((b, t, d), BF16),
        compiler_params=_cparams(_MM_SEM), name="q_proj",
    )(hn, w_qkv, q_norm.reshape(1, HEAD_DIM))
    wk_spec = _mm_specs(b, t, d, d, d)[4]
    k, kb = pl.pallas_call(
        _k_kernel, grid=grid, in_specs=[a_spec, wk_spec, nspec], out_specs=[o_spec, o_spec],
        out_shape=[jax.ShapeDtypeStruct((b, t, d), F32), jax.ShapeDtypeStruct((b, t, d), BF16)],
        compiler_params=_cparams(_MM_SEM), name="k_proj",
    )(hn, w_qkv, k_norm.reshape(1, HEAD_DIM))
    wv_spec = _mm_specs(b, t, d, d, 2 * d)[4]
    if v_transposed:
        vb_spec = pl.BlockSpec((1, tn, tm), lambda bi, i, j: (bi, j, i))
        vb_shape = jax.ShapeDtypeStruct((b, d, t), BF16)
    else:
        vb_spec, vb_shape = o_spec, jax.ShapeDtypeStruct((b, t, d), BF16)
    v, vb = pl.pallas_call(
        functools.partial(_v_kernel, transposed=v_transposed), grid=grid,
        in_specs=[a_spec, wv_spec], out_specs=[o_spec, vb_spec],
        out_shape=[jax.ShapeDtypeStruct((b, t, d), F32), vb_shape],
        compiler_params=_cparams(_MM_SEM), name="v_proj",
    )(hn, w_qkv)
    return q, k, kb, v, vb


def _attn_kernel(qi_ref, ki_ref, last_ref, slope_ref, lam_ref, q_ref, k_ref, v_ref, sub_ref, o_ref,
                 m1_ref, l1_ref, a1_ref, m2_ref, l2_ref, a2_ref, *, tq, tk, past, out_scale):
    h = pl.program_id(1)
    step = pl.program_id(2)
    qi = qi_ref[step]
    ki = ki_ref[step]
    slope = slope_ref[h]
    q0 = past + qi * tq
    k0 = ki * tk

    @pl.when(ki == 0)
    def _():
        for m_ref, l_ref, a_ref in ((m1_ref, l1_ref, a1_ref), (m2_ref, l2_ref, a2_ref)):
            m_ref[...] = jnp.full(m_ref.shape, NEG_INF, F32)
            l_ref[...] = jnp.zeros(l_ref.shape, F32)
            a_ref[...] = jnp.zeros(a_ref.shape, F32)

    def update(bias, visible):
        v = v_ref[0]
        for idx, (m_ref, l_ref, a_ref) in enumerate(((m1_ref, l1_ref, a1_ref), (m2_ref, l2_ref, a2_ref))):
            qm = q_ref[0, :, idx * HEAD_DIM:(idx + 1) * HEAD_DIM]
            km = k_ref[0, :, idx * HEAD_DIM:(idx + 1) * HEAD_DIM]
            s = lax.dot_general(qm, km, (((1,), (1,)), ((), ())), preferred_element_type=F32) + bias
            if visible is not None:
                s = jnp.where(visible, s, NEG_INF)
            m_old = m_ref[...]
            m_new = jnp.maximum(m_old, jnp.max(s, axis=-1, keepdims=True))
            alpha = jnp.exp2(m_old - m_new)
            p = jnp.exp2(s - m_new)
            l_ref[...] = alpha * l_ref[...] + jnp.sum(p, axis=-1, keepdims=True)
            a_ref[...] = alpha * a_ref[...] + jnp.dot(p.astype(v.dtype), v, preferred_element_type=F32)
            m_ref[...] = m_new

    strictly_past = (k0 + tk - 1) <= q0

    @pl.when(strictly_past)
    def _():
        kpos = k0 + lax.broadcasted_iota(jnp.int32, (1, tk), 1)
        update(slope * (kpos - q0).astype(F32), None)

    @pl.when(jnp.logical_not(strictly_past))
    def _():
        qpos = q0 + lax.broadcasted_iota(jnp.int32, (tq, tk), 0)
        kpos = k0 + lax.broadcasted_iota(jnp.int32, (tq, tk), 1)
        bias = slope * ((qpos - q0) - jnp.abs(qpos - kpos)).astype(F32)
        update(bias, (kpos // CHUNK) <= (qpos // CHUNK))

    @pl.when(last_ref[step] == 1)
    def _():
        o = a1_ref[...] / l1_ref[...] - lam_ref[0] * (a2_ref[...] / l2_ref[...])
        o = o * lax.rsqrt(jnp.mean(o * o, axis=-1, keepdims=True) + NORM_EPS) * sub_ref[...]
        o_ref[0] = (o * out_scale).astype(o_ref.dtype)


def _attn_tables(tq_len, tk_len, past, tq, tk):
    qi_l, ki_l, last_l = [], [], []
    for qi in range(tq_len // tq):
        q_last_chunk = (past + qi * tq + tq - 1) // CHUNK
        ks = [ki for ki in range(tk_len // tk) if (ki * tk) // CHUNK <= q_last_chunk]
        for ki in ks:
            qi_l.append(qi)
            ki_l.append(ki)
            last_l.append(1 if ki == ks[-1] else 0)
    return (np.asarray(qi_l, np.int32), np.asarray(ki_l, np.int32), np.asarray(last_l, np.int32))


def _diff_attention(q, k, v, lam, subln, past, lam_init, tq, tk):
    b, tq_len, d = q.shape
    tk_len = k.shape[1]
    n_heads = d // (2 * HEAD_DIM)
    hw = 2 * HEAD_DIM
    qi_t, ki_t, last_t = _attn_tables(tq_len, tk_len, past, tq, tk)
    slopes = jnp.asarray(LOG2E * 2.0 ** (-8.0 * np.arange(1, n_heads + 1, dtype=np.float32) / n_heads), F32)
    grid_spec = pltpu.PrefetchScalarGridSpec(
        num_scalar_prefetch=5,
        grid=(b, n_heads, len(qi_t)),
        in_specs=[pl.BlockSpec((1, tq, hw), lambda bi, h, s, qi, ki, *_: (bi, qi[s], h)),
                  pl.BlockSpec((1, tk, hw), lambda bi, h, s, qi, ki, *_: (bi, ki[s], h)),
                  pl.BlockSpec((1, tk, hw), lambda bi, h, s, qi, ki, *_: (bi, ki[s], h)),
                  pl.BlockSpec((1, hw), lambda bi, h, s, *_: (0, 0))],
        out_specs=pl.BlockSpec((1, tq, hw), lambda bi, h, s, qi, ki, *_: (bi, qi[s], h)),
        scratch_shapes=[pltpu.VMEM((tq, 1), F32), pltpu.VMEM((tq, 1), F32), pltpu.VMEM((tq, hw), F32),
                        pltpu.VMEM((tq, 1), F32), pltpu.VMEM((tq, 1), F32), pltpu.VMEM((tq, hw), F32)],
    )
    return pl.pallas_call(
        functools.partial(_attn_kernel, tq=tq, tk=tk, past=past, out_scale=1.0 - lam_init),
        grid_spec=grid_spec,
        out_shape=jax.ShapeDtypeStruct((b, tq_len, d), BF16),
        compiler_params=_cparams(("arbitrary", "arbitrary", "arbitrary")), name="diff_attn",
    )(jnp.asarray(qi_t), jnp.asarray(ki_t), jnp.asarray(last_t), slopes, lam.reshape(1).astype(F32),
      q, k, v, subln.reshape(1, hw).astype(F32))


def _attn_t_kernel(qi_ref, ki_ref, last_ref, slope_ref, lam_ref, q_ref, k_ref, vt_ref, sub_ref, kp_ref, o_ref,
                   m_ref, l_ref, a_ref, *, tq, tk, out_scale):
    h = pl.program_id(1)
    step = pl.program_id(2)
    qi = qi_ref[step]
    ki = ki_ref[step]
    slope = slope_ref[h]
    q0 = qi * tq
    k0 = ki * tk
    lane_tiles = tq // 128

    @pl.when(ki == 0)
    def _():
        m_ref[...] = jnp.full(m_ref.shape, NEG_INF, F32)
        l_ref[...] = jnp.zeros(l_ref.shape, F32)
        a_ref[...] = jnp.zeros(a_ref.shape, F32)

    def update(bias_fn):
        vt = vt_ref[0]
        for idx in range(2):
            qm = q_ref[0, :, idx * HEAD_DIM:(idx + 1) * HEAD_DIM]
            km = k_ref[0, :, idx * HEAD_DIM:(idx + 1) * HEAD_DIM]
            st = bias_fn(lax.dot_general(km, qm, (((1,), (1,)), ((), ())), preferred_element_type=F32))
            m_old = m_ref[idx]
            m_new = jnp.maximum(m_old, jnp.max(st, axis=0, keepdims=True))
            alpha = jnp.exp2(m_old - m_new)
            pt = jnp.exp2(st - m_new)
            l_ref[idx] = alpha * l_ref[idx] + jnp.sum(pt, axis=0, keepdims=True)
            a_ref[idx] = alpha * a_ref[idx] + jnp.dot(vt, pt.astype(vt.dtype), preferred_element_type=F32)
            m_ref[idx] = m_new

    strictly_past = (k0 + tk - 1) <= q0

    @pl.when(strictly_past)
    def _():
        kb = slope * (kp_ref[...] - q0.astype(F32))
        update(lambda st: jnp.concatenate(
            [st[:, c * 128:(c + 1) * 128] + kb for c in range(lane_tiles)], axis=1))

    @pl.when(jnp.logical_not(strictly_past))
    def _():
        def bias_fn(st):
            kpos = k0 + lax.broadcasted_iota(jnp.int32, (tk, tq), 0)
            qpos = q0 + lax.broadcasted_iota(jnp.int32, (tk, tq), 1)
            bias = slope * ((qpos - q0) - jnp.abs(qpos - kpos)).astype(F32)
            return jnp.where((kpos // CHUNK) <= (qpos // CHUNK), st + bias, NEG_INF)
        update(bias_fn)

    @pl.when(last_ref[step] == 1)
    def _():
        o = a_ref[0] / l_ref[0] - lam_ref[0] * (a_ref[1] / l_ref[1])
        o = o * lax.rsqrt(jnp.mean(o * o, axis=0, keepdims=True) + NORM_EPS)
        o = jnp.concatenate([o[:, c * 128:(c + 1) * 128] * sub_ref[...] for c in range(lane_tiles)], axis=1)
        o_ref[0] = (o * out_scale).T.astype(o_ref.dtype)


def _diff_attention_t(q, k, vt, lam, subln, lam_init, tq, tk):
    b, t, d = q.shape
    n_heads = d // (2 * HEAD_DIM)
    hw = 2 * HEAD_DIM
    qi_t, ki_t, last_t = _attn_tables(t, t, 0, tq, tk)
    slopes = jnp.asarray(LOG2E * 2.0 ** (-8.0 * np.arange(1, n_heads + 1, dtype=np.float32) / n_heads), F32)
    key_pos = jnp.broadcast_to(jnp.arange(t, dtype=F32)[:, None], (t, 128))
    sub_rows = jnp.broadcast_to(subln.astype(F32).reshape(hw, 1), (hw, 128))
    grid_spec = pltpu.PrefetchScalarGridSpec(
        num_scalar_prefetch=5,
        grid=(b, n_heads, len(qi_t)),
        in_specs=[pl.BlockSpec((1, tq, hw), lambda bi, h, s, qi, ki, *_: (bi, qi[s], h)),
                  pl.BlockSpec((1, tk, hw), lambda bi, h, s, qi, ki, *_: (bi, ki[s], h)),
                  pl.BlockSpec((1, hw, tk), lambda bi, h, s, qi, ki, *_: (bi, h, ki[s])),
                  pl.BlockSpec((hw, 128), lambda bi, h, s, *_: (0, 0)),
                  pl.BlockSpec((tk, 128), lambda bi, h, s, qi, ki, *_: (ki[s], 0))],
        out_specs=pl.BlockSpec((1, tq, hw), lambda bi, h, s, qi, ki, *_: (bi, qi[s], h)),
        scratch_shapes=[pltpu.VMEM((2, 1, tq), F32), pltpu.VMEM((2, 1, tq), F32), pltpu.VMEM((2, hw, tq), F32)],
    )
    return pl.pallas_call(
        functools.partial(_attn_t_kernel, tq=tq, tk=tk, out_scale=1.0 - lam_init),
        grid_spec=grid_spec,
        out_shape=jax.ShapeDtypeStruct((b, t, d), BF16),
        compiler_params=_cparams(("arbitrary", "arbitrary", "arbitrary")), name="diff_attn_t",
    )(jnp.asarray(qi_t), jnp.asarray(ki_t), jnp.asarray(last_t), slopes, lam.reshape(1).astype(F32),
      q, k, vt, sub_rows, key_pos)


def _moe_kernel(be_ref, nu_ref, xs_ref, w1_ref, w3_ref, w2_ref, o_ref):
    blk = pl.program_id(0)

    @pl.when(blk < nu_ref[0])
    def _():
        x = xs_ref[...]
        h1 = jnp.dot(x, w1_ref[0, 0], preferred_element_type=F32)
        h3 = jnp.dot(x, w3_ref[0, 0], preferred_element_type=F32)
        hid = (h1 * jax.nn.sigmoid(h1)) * h3
        o_ref[...] = jnp.dot(hid.astype(BF16), w2_ref[0, 0], preferred_element_type=F32)

    @pl.when(blk >= nu_ref[0])
    def _():
        o_ref[...] = jnp.zeros(o_ref.shape, o_ref.dtype)


def _top2(x, axis):
    pos = lax.broadcasted_iota(jnp.int32, x.shape, axis)
    i0 = jnp.argmax(x, axis=axis).astype(jnp.int32)
    v0 = jnp.max(x, axis=axis)
    rest = jnp.where(pos == jnp.expand_dims(i0, axis), -jnp.inf, x)
    i1 = jnp.argmax(rest, axis=axis).astype(jnp.int32)
    v1 = jnp.max(rest, axis=axis)
    return (v0, v1), (i0, i1)


def _route(logits, router_b):
    n, e = logits.shape
    per = e // N_EXPERT_GROUPS
    scores = jax.nn.sigmoid(logits.T)
    sel_g = (scores + router_b.astype(F32)[:, None]).reshape(N_EXPERT_GROUPS, per, n)
    (v0, v1), _ = _top2(sel_g, 1)
    g_idx = jnp.argmax(v0 + v1, axis=0).astype(jnp.int32)
    in_group = jnp.take_along_axis(sel_g, g_idx[None, None, :], axis=0)[0]
    _, (l0, l1) = _top2(in_group, 0)
    e0, e1 = g_idx * per + l0, g_idx * per + l1
    w0 = jnp.take_along_axis(scores, e0[None, :], axis=0)[0]
    w1 = jnp.take_along_axis(scores, e1[None, :], axis=0)[0]
    total = w0 + w1
    return jnp.stack([e0, e1], axis=-1), jnp.stack([w0 / total, w1 / total], axis=-1)


def _cumsum_rows(mask):
    n, e = mask.shape
    rb = _tile(n, 128)
    m3 = mask.reshape(n // rb, rb, e).astype(BF16)
    tri = jnp.tril(jnp.ones((rb, rb), BF16))
    within = jnp.einsum('ij,bje->bie', tri, m3, preferred_element_type=F32).astype(jnp.int32)
    totals = within[:, -1, :]
    offs = jnp.cumsum(totals, axis=0) - totals
    return (within + offs[:, None, :]).reshape(n, e)


def _moe(hn, logits, router_b, w1, w3, w2, layer, bm):
    n, d = hn.shape
    e, f = w1.shape[1], w1.shape[3]
    expert_idx, gate = _route(logits, router_b)
    n_assign = n * TOP_K
    flat_e = expert_idx.reshape(n_assign)
    onehot = flat_e[:, None] == jnp.arange(e, dtype=jnp.int32)[None, :]
    csum = _cumsum_rows(onehot)
    counts = csum[-1]
    rank = jnp.take_along_axis(csum, flat_e[:, None], axis=1)[:, 0] - 1
    padded = (counts + bm - 1) // bm * bm
    pad_end = jnp.cumsum(padded)
    pad_start = pad_end - padded
    dest = pad_start[flat_e] + rank
    n_blocks = -(-n_assign // bm) + e
    slot_tok = jnp.zeros((n_blocks * bm,), jnp.int32).at[dest].set(
        jnp.arange(n_assign, dtype=jnp.int32) // TOP_K)
    xs = hn[slot_tok]
    block_e = jnp.minimum(
        jnp.searchsorted(pad_end, jnp.arange(n_blocks, dtype=jnp.int32) * bm, side='right'),
        e - 1).astype(jnp.int32)
    n_used = (pad_end[-1] // bm).astype(jnp.int32).reshape(1)
    once = pl.Buffered(1)
    grid_spec = pltpu.PrefetchScalarGridSpec(
        num_scalar_prefetch=2,
        grid=(n_blocks,),
        in_specs=[pl.BlockSpec((bm, d), lambda bi, be, nu: (bi, 0)),
                  pl.BlockSpec((1, 1, d, f), lambda bi, be, nu: (layer, be[bi], 0, 0), pipeline_mode=once),
                  pl.BlockSpec((1, 1, d, f), lambda bi, be, nu: (layer, be[bi], 0, 0), pipeline_mode=once),
                  pl.BlockSpec((1, 1, f, d), lambda bi, be, nu: (layer, be[bi], 0, 0), pipeline_mode=once)],
        out_specs=pl.BlockSpec((bm, d), lambda bi, be, nu: (bi, 0)),
    )
    ys = pl.pallas_call(
        _moe_kernel, grid_spec=grid_spec,
        out_shape=jax.ShapeDtypeStruct((n_blocks * bm, d), F32),
        compiler_params=_cparams(("arbitrary",)), name="moe_experts",
    )(block_e, n_used, xs, w1, w3, w2)
    dest = dest.reshape(n, TOP_K)
    return gate[:, 0:1] * ys[dest[:, 0]] + gate[:, 1:2] * ys[dest[:, 1]]


def kernel(x_prompt, x_sample, c_prompt, c_sample, state_ssm_re, state_ssm_im, cache_k, cache_v, ada_w, ada_b, norm_mix, norm_ffn, ssm_a_re, ssm_a_im, ssm_log_dt, ssm_b_re, ssm_b_im, ssm_c_re, ssm_c_im, ssm_d, ssm_glu_w, ssm_glu_b, attn_w_qkv, attn_w_o, attn_q_norm, attn_k_norm, attn_lambda, attn_subln, router_w, router_b, moe_w1, moe_w3, moe_w2):
    depth, d = norm_mix.shape
    n_mixers = 2
    bp, bs = x_prompt.shape[0], x_sample.shape[0]
    assert bp == 1 and bs == SUBLANES, "the S5 scan maps one long sequence or eight sequences onto sublanes"
    n_heads = d // (2 * HEAD_DIM)
    g, p = ssm_a_re.shape[1:]
    past_len = cache_k.shape[2]

    r_all = -(-(bp + bs) // SUBLANES) * SUBLANES
    c_all = jnp.concatenate([c_prompt, c_sample, jnp.zeros((r_all - bp - bs, d), F32)])
    mod_all = _ada_mod(c_all, ada_w, ada_b).reshape(depth, r_all, 6, d)

    glu_w = _to_bf16(ssm_glu_w)
    w_qkv = _to_bf16(attn_w_qkv)
    w_o = _to_bf16(attn_w_o)
    w1, w3, w2 = _to_bf16(moe_w1), _to_bf16(moe_w3), _to_bf16(moe_w2)
    router_bf = router_w.astype(BF16)

    def trunk(x, row0, h0_re, h0_im, past_k, past_v, chained, bm):
        b, t, _ = x.shape
        new_re, new_im, new_k, new_v = [], [], [], []
        for i in range(depth):
            mod = mod_all[i, row0:row0 + b][:, :, None, :]
            shift1, scale1, gate1, shift2, scale2, gate2 = (mod[:, m] for m in range(6))
            j = i // n_mixers
            hn = _norm_mod(x, norm_mix[i], scale1, shift1, out_dtype=F32 if i % n_mixers == 0 else BF16)
            if i % n_mixers == 0:
                seg_len = b * (t // S5_L) // SUBLANES
                prm = _s5_params(ssm_a_re[j], ssm_a_im[j], ssm_log_dt[j], ssm_b_re[j], ssm_b_im[j],
                                 ssm_c_re[j], ssm_c_im[j], ssm_d[j], seg_len)
                if chained:
                    pad = jnp.zeros((SUBLANES - 1, g * p), F32)
                    i_re = jnp.concatenate([h0_re[j].reshape(1, g * p), pad])
                    i_im = jnp.concatenate([h0_im[j].reshape(1, g * p), pad])
                else:
                    i_re, i_im = h0_re[j].reshape(b, g * p), h0_im[j].reshape(b, g * p)
                gl, f_re, f_im = _s5_mixer(hn, i_re, i_im, prm, chained)
                if chained:
                    f_re, f_im = f_re[SUBLANES - 1:], f_im[SUBLANES - 1:]
                new_re.append(f_re.reshape(b, g, p))
                new_im.append(f_im.reshape(b, g, p))
                x = _glu_proj(gl, glu_w[j], ssm_glu_b[j], x, gate1)
            else:
                lam_init = 0.8 - 0.6 * math.exp(-0.3 * i)
                q, k, kb, v, vb = _qkv_proj(hn, w_qkv[j], attn_q_norm[j], attn_k_norm[j], past_k is None)
                new_k.append(k.reshape(b, t, n_heads, 2, HEAD_DIM))
                new_v.append(v.reshape(b, t, n_heads, 2 * HEAD_DIM))
                lp = attn_lambda[j].astype(F32)
                lam = jnp.exp(jnp.sum(lp[0] * lp[1])) - jnp.exp(jnp.sum(lp[2] * lp[3])) + lam_init
                if past_k is None:
                    o = _diff_attention_t(q, kb, vb, lam, attn_subln[j], lam_init,
                                          _tile(t, ATTN_TQ), _tile(t, ATTN_TK))
                else:
                    past = past_k.shape[2]
                    tk = _tile(past, ATTN_TK // 2)
                    fill = jnp.zeros((b, tk - t % tk if t % tk else 0, d), BF16)
                    k_all = jnp.concatenate([past_k[j].reshape(b, past, d).astype(BF16), kb, fill], axis=1)
                    v_all = jnp.concatenate([past_v[j].reshape(b, past, d).astype(BF16), vb, fill], axis=1)
                    o = _diff_attention(q, k_all, v_all, lam, attn_subln[j], past, lam_init, t, tk)
                x = _out_proj(o, w_o[j], x, gate1)
            hn, logits = _norm_mod(x, norm_ffn[i], scale2, shift2, router_bf)
            y = _moe(hn.reshape(b * t, d), logits.reshape(b * t, -1), router_b, w1, w3, w2, i, bm)
            x = x + gate2 * y.reshape(b, t, d)
        return x, jnp.stack(new_re), jnp.stack(new_im), jnp.stack(new_k), jnp.stack(new_v)

    zero_state = jnp.zeros((state_ssm_re.shape[0], bp, g, p), F32)
    y_p, re_p, im_p, k_p, v_p = trunk(x_prompt, 0, zero_state, zero_state, None, None, True, 256)
    y_s, re_s, im_s, k_s, v_s = trunk(x_sample, bp, state_ssm_re, state_ssm_im, cache_k, cache_v, False, 128)
    return (y_p, y_s, re_p, im_p, k_p, v_p, re_s, im_s, k_s, v_s)
```

```python
import functools
import math

import numpy as np
import jax
import jax.numpy as jnp
from jax import lax
from jax.experimental import pallas as pl
from jax.experimental.pallas import tpu as pltpu

F32 = jnp.float32
BF16 = jnp.bfloat16
HIGHEST = lax.Precision.HIGHEST

NORM_EPS = 1e-6
CHUNK = 64
SSM_GROUP = 16
SSM_STATE = 64
S5_L = 16
S5_GB = 8
HEAD_DIM = 128
N_EXPERT_GROUPS = 8
TOP_K = 2
NEG_INF = -1e30
LOG2E = 1.4426950408889634
ATTN_TQ = 512
ATTN_TK = 2048
CAST_BLOCK_ELEMS = 2 * 1024 * 1024
SUBLANES = 8
VMEM_LIMIT = 52 * 1024 * 1024
MOE_VMEM_LIMIT = 58 * 1024 * 1024


def _cparams(sem):
    return pltpu.CompilerParams(dimension_semantics=sem, vmem_limit_bytes=VMEM_LIMIT)


def _tile(n, pref):
    if n <= pref:
        return n
    t = pref
    while n % t:
        t //= 2
    return t


def _cast_kernel(x_ref, o_ref):
    o_ref[...] = x_ref[...].astype(o_ref.dtype)


def _to_bf16(w):
    cols = w.shape[-1]
    rows = w.size // cols
    tr = _tile(rows, 1 << int(math.log2(CAST_BLOCK_ELEMS // cols)))
    spec = pl.BlockSpec((tr, cols), lambda i: (i, 0))
    out = pl.pallas_call(
        _cast_kernel, grid=(rows // tr,), in_specs=[spec], out_specs=spec,
        out_shape=jax.ShapeDtypeStruct((rows, cols), BF16),
        compiler_params=_cparams(("arbitrary",)), name="to_bf16",
    )(w.reshape(rows, cols))
    return out.reshape(w.shape)


def _ada_kernel(c_ref, w_ref, b_ref, o_ref):
    c = c_ref[...]
    s = (c * jax.nn.sigmoid(c)).astype(BF16)
    o_ref[0] = jnp.dot(s, w_ref[0].astype(BF16), preferred_element_type=F32) + b_ref[0]


def _ada_mod(c_all, ada_w, ada_b):
    depth, d, n6 = ada_w.shape
    r = c_all.shape[0]
    tn = _tile(n6, 512)
    return pl.pallas_call(
        _ada_kernel,
        grid=(depth, n6 // tn),
        in_specs=[pl.BlockSpec((r, d), lambda i, j: (0, 0)),
                  pl.BlockSpec((1, d, tn), lambda i, j: (i, 0, j)),
                  pl.BlockSpec((1, 1, tn), lambda i, j: (i, 0, j))],
        out_specs=pl.BlockSpec((1, r, tn), lambda i, j: (i, 0, j)),
        out_shape=jax.ShapeDtypeStruct((depth, r, n6), F32),
        compiler_params=_cparams(("arbitrary", "arbitrary")),
        name="ada_mod",
    )(c_all, ada_w, ada_b.reshape(depth, 1, n6))


def _normmod_kernel(x_ref, g_ref, sc_ref, sh_ref, o_ref):
    x = x_ref[0]
    y = x * lax.rsqrt(jnp.mean(x * x, axis=-1, keepdims=True) + NORM_EPS)
    o_ref[0] = (y * g_ref[...] * (1.0 + sc_ref[0]) + sh_ref[0]).astype(o_ref.dtype)


def _normmod_router_kernel(x_ref, g_ref, sc_ref, sh_ref, rw_ref, o_ref, lg_ref):
    x = x_ref[0]
    y = x * lax.rsqrt(jnp.mean(x * x, axis=-1, keepdims=True) + NORM_EPS)
    hn = (y * g_ref[...] * (1.0 + sc_ref[0]) + sh_ref[0]).astype(o_ref.dtype)
    o_ref[0] = hn
    lg_ref[0] = jnp.dot(hn, rw_ref[...], preferred_element_type=F32)


def _norm_mod(x, g, scale, shift, router_w=None, out_dtype=BF16):
    b, t, d = x.shape
    tm = _tile(t, 256)
    xspec = pl.BlockSpec((1, tm, d), lambda i, j: (i, j, 0))
    vspec = pl.BlockSpec((1, 1, d), lambda i, j: (i, 0, 0))
    gspec = pl.BlockSpec((1, d), lambda i, j: (0, 0))
    if router_w is None:
        return pl.pallas_call(
            _normmod_kernel, grid=(b, t // tm),
            in_specs=[xspec, gspec, vspec, vspec], out_specs=xspec,
            out_shape=jax.ShapeDtypeStruct((b, t, d), out_dtype),
            compiler_params=_cparams(("arbitrary", "arbitrary")), name="norm_mod",
        )(x, g.reshape(1, d), scale, shift)
    e = router_w.shape[1]
    return pl.pallas_call(
        _normmod_router_kernel, grid=(b, t // tm),
        in_specs=[xspec, gspec, vspec, vspec, pl.BlockSpec((d, e), lambda i, j: (0, 0))],
        out_specs=[xspec, pl.BlockSpec((1, tm, e), lambda i, j: (i, j, 0))],
        out_shape=[jax.ShapeDtypeStruct((b, t, d), BF16), jax.ShapeDtypeStruct((b, t, e), F32)],
        compiler_params=_cparams(("arbitrary", "arbitrary")), name="norm_mod_router",
    )(x, g.reshape(1, d), scale, shift, router_w)


def _cmul(ar, ai, br, bi):
    return ar * br - ai * bi, ar * bi + ai * br


def _cpow(re, im, n):
    out_re, out_im = jnp.ones_like(re), jnp.zeros_like(im)
    while n:
        if n & 1:
            out_re, out_im = _cmul(out_re, out_im, re, im)
        re, im = _cmul(re, im, re, im)
        n >>= 1
    return out_re, out_im


def _s5_params(a_re, a_im, log_dt, b_re, b_im, c_re, c_im, d, seg_len):
    g, p = a_re.shape
    h = SSM_GROUP
    L = S5_L
    q = g // S5_GB
    dt = jnp.exp(log_dt.astype(F32))[:, None]
    mag = jnp.exp(a_re * dt)
    abar_re = mag * jnp.cos(a_im * dt)
    abar_im = mag * jnp.sin(a_im * dt)
    den = a_re * a_re + a_im * a_im
    nr = abar_re - 1.0
    coef_re = (nr * a_re + abar_im * a_im) / den
    coef_im = (abar_im * a_re - nr * a_im) / den
    bbar_re = coef_re[..., None] * b_re - coef_im[..., None] * b_im
    bbar_im = coef_re[..., None] * b_im + coef_im[..., None] * b_re
    pw_re, pw_im = [jnp.ones_like(abar_re)], [jnp.zeros_like(abar_im)]
    for _ in range(L):
        r_, i_ = _cmul(pw_re[-1], pw_im[-1], abar_re, abar_im)
        pw_re.append(r_)
        pw_im.append(i_)
    pw_re = jnp.stack(pw_re)
    pw_im = jnp.stack(pw_im)
    w1_re = pw_re[:L, :, :, None] * bbar_re[None] - pw_im[:L, :, :, None] * bbar_im[None]
    w1_im = pw_re[:L, :, :, None] * bbar_im[None] + pw_im[:L, :, :, None] * bbar_re[None]
    k_tau = (jnp.einsum('gop,tgpi->tgio', c_re, w1_re, precision=HIGHEST)
             - jnp.einsum('gop,tgpi->tgio', c_im, w1_im, precision=HIGHEST))
    gb = S5_GB
    eye = jnp.eye(gb, dtype=F32)
    kcat = jnp.einsum('tqgio,gk->qgitko', k_tau.reshape(L, q, gb, h, h), eye).reshape(q, gb * h, L * gb * h)
    ws_re = jnp.einsum('lqgpi,gk->qlgikp', w1_re[::-1].reshape(L, q, gb, p, h), eye).reshape(q, L * gb * h, gb * p)
    ws_im = jnp.einsum('lqgpi,gk->qlgikp', w1_im[::-1].reshape(L, q, gb, p, h), eye).reshape(q, L * gb * h, gb * p)
    ws = jnp.concatenate([ws_re, ws_im], axis=-1)
    x_re = c_re[None] * pw_re[1:, :, None, :] - c_im[None] * pw_im[1:, :, None, :]
    x_im = c_re[None] * pw_im[1:, :, None, :] + c_im[None] * pw_re[1:, :, None, :]
    wy_re = jnp.einsum('lqgop,gk->qgplko', x_re.reshape(L, q, gb, h, p), eye).reshape(q, gb * p, L * gb * h)
    wy_im = jnp.einsum('lqgop,gk->qgplko', -x_im.reshape(L, q, gb, h, p), eye).reshape(q, gb * p, L * gb * h)
    wy = jnp.concatenate([wy_re, wy_im], axis=1)
    achunk_re, achunk_im = pw_re[L], pw_im[L]
    aseg_re, aseg_im = _cpow(achunk_re, achunk_im, seg_len)
    nk = gb * p // 128
    lanes = lambda a: a.reshape(q, nk, 1, 128)
    return dict(kcat=kcat.astype(BF16), ws=ws.astype(BF16), wy=wy.astype(BF16),
                ac_re=lanes(achunk_re), ac_im=lanes(achunk_im), as_re=lanes(aseg_re), as_im=lanes(aseg_im),
                d=d.reshape(q, 1, gb * h))


def _s5_state_kernel(x_ref, ws_ref, ac_re_ref, ac_im_ref, as_re_ref, as_im_ref, i_re_ref, i_im_ref,
                     hre_ref, him_ref, fre_ref, fim_ref, ucat_ref, *, seg_len, chained, nc, pb):
    for l in range(S5_L):
        ucat_ref[:, l * 128:(l + 1) * 128] = x_ref[pl.ds(l, nc, stride=S5_L), :].astype(BF16)
    s_all = jnp.dot(ucat_ref[...], ws_ref[0], preferred_element_type=F32)
    for j in range(pb):
        hre_ref[0, j] = s_all[:, j * 128:(j + 1) * 128]
        him_ref[0, j] = s_all[:, (pb + j) * 128:(pb + j + 1) * 128]

    shape = (SUBLANES, 128)
    ar = [jnp.broadcast_to(ac_re_ref[0, j], shape) for j in range(pb)]
    ai = [jnp.broadcast_to(ac_im_ref[0, j], shape) for j in range(pb)]

    def advance(i, carry):
        out = []
        for j in range(pb):
            hr, hi = carry[2 * j], carry[2 * j + 1]
            out.append(ar[j] * hr - ai[j] * hi + hre_ref[0, j, pl.ds(i, SUBLANES, stride=seg_len), :])
            out.append(ar[j] * hi + ai[j] * hr + him_ref[0, j, pl.ds(i, SUBLANES, stride=seg_len), :])
        return tuple(out)

    def store_advance(i, carry):
        new = advance(i, carry)
        for j in range(pb):
            hre_ref[0, j, pl.ds(i, SUBLANES, stride=seg_len), :] = carry[2 * j]
            him_ref[0, j, pl.ds(i, SUBLANES, stride=seg_len), :] = carry[2 * j + 1]
        return new

    start = []
    for j in range(pb):
        start += [i_re_ref[0, j], i_im_ref[0, j]]
    start = tuple(start)
    if chained:
        ends = lax.fori_loop(0, seg_len, advance, start)
        row = lax.broadcasted_iota(jnp.int32, shape, 0)
        fixed = []
        for j in range(pb):
            er, ei = ends[2 * j], ends[2 * j + 1]
            sr, si = as_re_ref[0, j], as_im_ref[0, j]
            tr, ti = er[0:1], ei[0:1]
            t_re, t_im = start[2 * j], start[2 * j + 1]
            for s in range(1, SUBLANES):
                if s > 1:
                    tr, ti = sr * tr - si * ti + er[s - 1:s], sr * ti + si * tr + ei[s - 1:s]
                t_re = jnp.where(row == s, jnp.broadcast_to(tr, shape), t_re)
                t_im = jnp.where(row == s, jnp.broadcast_to(ti, shape), t_im)
            fixed += [t_re, t_im]
        start = tuple(fixed)
    final = lax.fori_loop(0, seg_len, store_advance, start)
    for j in range(pb):
        fre_ref[0, j] = final[2 * j]
        fim_ref[0, j] = final[2 * j + 1]


def _s5_out_kernel(x_ref, kcat_ref, wy_ref, hre_ref, him_ref, d_ref, o_ref, y_ref, *, ncb, pb):
    hp = jnp.concatenate([hre_ref[0, j] for j in range(pb)] + [him_ref[0, j] for j in range(pb)],
                         axis=-1).astype(BF16)
    y_ref[...] = jnp.dot(hp, wy_ref[0], preferred_element_type=F32)
    for l_in in range(S5_L):
        u = x_ref[pl.ds(l_in, ncb, stride=S5_L), :].astype(BF16)
        width = (S5_L - l_in) * 128
        y_ref[:, l_in * 128:] += jnp.dot(u, kcat_ref[0, :, :width], preferred_element_type=F32)
    for l_out in range(S5_L):
        y = y_ref[:, l_out * 128:(l_out + 1) * 128] + d_ref[0] * x_ref[pl.ds(l_out, ncb, stride=S5_L), :]
        o_ref[pl.ds(l_out, ncb, stride=S5_L), :] = jax.nn.gelu(y, approximate=True)


def _s5_mixer(hn, h0_re, h0_im, prm, chained):
    b, t, d = hn.shape
    w = d // SSM_GROUP * SSM_STATE
    q = d // 128
    pb = S5_GB * SSM_STATE // 128
    rows = b * t
    nc = rows // S5_L
    seg_len = nc // SUBLANES
    x2 = hn.reshape(rows, d)
    lanes = lambda a: a.reshape(SUBLANES, q, pb, 128).transpose(1, 2, 0, 3)
    hblk = pl.BlockSpec((1, pb, nc, 128), lambda i: (i, 0, 0, 0))
    row = pl.BlockSpec((1, pb, 1, 128), lambda i: (i, 0, 0, 0))
    st = pl.BlockSpec((1, pb, SUBLANES, 128), lambda i: (i, 0, 0, 0))
    h_re, h_im, f_re, f_im = pl.pallas_call(
        functools.partial(_s5_state_kernel, seg_len=seg_len, chained=chained, nc=nc, pb=pb), grid=(q,),
        in_specs=[pl.BlockSpec((rows, 128), lambda i: (0, i)),
                  pl.BlockSpec((1, S5_L * 128, 2 * pb * 128), lambda i: (i, 0, 0)),
                  row, row, row, row, st, st],
        out_specs=[hblk, hblk, st, st],
        out_shape=[jax.ShapeDtypeStruct((q, pb, nc, 128), F32)] * 2
        + [jax.ShapeDtypeStruct((q, pb, SUBLANES, 128), F32)] * 2,
        scratch_shapes=[pltpu.VMEM((nc, S5_L * 128), BF16)],
        compiler_params=_cparams(("arbitrary",)), name="s5_state",
    )(x2, prm['ws'], prm['ac_re'], prm['ac_im'], prm['as_re'], prm['as_im'], lanes(h0_re), lanes(h0_im))
    ncb = _tile(nc, 256)
    hspec = pl.BlockSpec((1, pb, ncb, 128), lambda i, r: (i, 0, r, 0))
    xspec = pl.BlockSpec((ncb * S5_L, 128), lambda i, r: (r, i))
    gl = pl.pallas_call(
        functools.partial(_s5_out_kernel, ncb=ncb, pb=pb), grid=(q, nc // ncb),
        in_specs=[xspec,
                  pl.BlockSpec((1, 128, S5_L * 128), lambda i, r: (i, 0, 0)),
                  pl.BlockSpec((1, 2 * pb * 128, S5_L * 128), lambda i, r: (i, 0, 0)),
                  hspec, hspec,
                  pl.BlockSpec((1, 1, 128), lambda i, r: (i, 0, 0))],
        out_specs=xspec,
        out_shape=jax.ShapeDtypeStruct((rows, d), F32),
        scratch_shapes=[pltpu.VMEM((ncb, S5_L * 128), F32)],
        compiler_params=_cparams(("arbitrary", "arbitrary")), name="s5_out",
    )(x2, prm['kcat'], prm['wy'], h_re, h_im, prm['d'])
    unlanes = lambda a: a.transpose(2, 0, 1, 3).reshape(SUBLANES, w)
    return gl.reshape(b, t, d), unlanes(f_re), unlanes(f_im)


def _glu_kernel(g_ref, w_ref, b_ref, gt_ref, x_ref, gate_ref, o_ref, gb_ref):
    @pl.when(pl.program_id(2) == 0)
    def _():
        gb_ref[...] = g_ref[0].astype(BF16)

    acc = jnp.dot(gb_ref[...], w_ref[...], preferred_element_type=F32) + b_ref[...]
    o_ref[0] = x_ref[0] + gate_ref[0] * (gt_ref[0] * jax.nn.sigmoid(acc))


def _proj_res_kernel(a_ref, w_ref, x_ref, gate_ref, o_ref):
    acc = jnp.dot(a_ref[0], w_ref[...], preferred_element_type=F32)
    o_ref[0] = x_ref[0] + gate_ref[0] * acc


def _head_rms(acc, gain):
    outs = []
    for s in range(acc.shape[1] // HEAD_DIM):
        a = acc[:, s * HEAD_DIM:(s + 1) * HEAD_DIM]
        outs.append(a * lax.rsqrt(jnp.mean(a * a, axis=-1, keepdims=True) + NORM_EPS) * gain)
    return jnp.concatenate(outs, axis=-1) if len(outs) > 1 else outs[0]


def _q_kernel(a_ref, w_ref, n_ref, o_ref):
    acc = jnp.dot(a_ref[0], w_ref[...], preferred_element_type=F32)
    o_ref[0] = (_head_rms(acc, n_ref[...]) * (LOG2E * HEAD_DIM ** -0.5)).astype(o_ref.dtype)


def _k_kernel(a_ref, w_ref, n_ref, o_ref, ob_ref):
    acc = jnp.dot(a_ref[0], w_ref[...], preferred_element_type=F32)
    k = _head_rms(acc, n_ref[...])
    o_ref[0] = k
    ob_ref[0] = k.astype(ob_ref.dtype)


def _v_kernel(a_ref, w_ref, o_ref, ob_ref, *, transposed):
    acc = jnp.dot(a_ref[0], w_ref[...], preferred_element_type=F32)
    o_ref[0] = acc
    ob_ref[0] = (acc.T if transposed else acc).astype(ob_ref.dtype)


def _mm_specs(b, t, k, n, col0, tm_pref=1024, tn_pref=512):
    tm = _tile(t, tm_pref)
    tn = _tile(n, tn_pref)
    grid = (b, t // tm, n // tn)
    a_spec = pl.BlockSpec((1, tm, k), lambda bi, i, j: (bi, i, 0))
    w_spec = pl.BlockSpec((k, tn), lambda bi, i, j: (0, j + col0 // tn))
    o_spec = pl.BlockSpec((1, tm, tn), lambda bi, i, j: (bi, i, j))
    return tm, tn, grid, a_spec, w_spec, o_spec


_MM_SEM = ("arbitrary", "arbitrary", "arbitrary")


def _glu_proj(gl, w, bias, x, gate):
    b, t, d = gl.shape
    tm, tn, grid, a_spec, w_spec, o_spec = _mm_specs(b, t, d, d, 0, tm_pref=512)
    return pl.pallas_call(
        _glu_kernel, grid=grid,
        in_specs=[a_spec, w_spec, pl.BlockSpec((1, tn), lambda bi, i, j: (0, j)), o_spec, o_spec,
                  pl.BlockSpec((1, 1, tn), lambda bi, i, j: (bi, 0, j))],
        out_specs=o_spec, out_shape=jax.ShapeDtypeStruct((b, t, d), F32),
        scratch_shapes=[pltpu.VMEM((tm, d), BF16)],
        compiler_params=_cparams(_MM_SEM), name="glu_proj",
    )(gl, w, bias.reshape(1, d), gl, x, gate)


def _out_proj(a, w, x, gate):
    b, t, d = a.shape
    tm, tn, grid, a_spec, w_spec, o_spec = _mm_specs(b, t, d, d, 0)
    return pl.pallas_call(
        _proj_res_kernel, grid=grid,
        in_specs=[a_spec, w_spec, o_spec, pl.BlockSpec((1, 1, tn), lambda bi, i, j: (bi, 0, j))],
        out_specs=o_spec, out_shape=jax.ShapeDtypeStruct((b, t, d), F32),
        compiler_params=_cparams(_MM_SEM), name="out_proj",
    )(a, w, x, gate)


def _qkv_proj(hn, w_qkv, q_norm, k_norm, v_transposed):
    b, t, d = hn.shape
    nspec = pl.BlockSpec((1, HEAD_DIM), lambda bi, i, j: (0, 0))
    tm, tn, grid, a_spec, wq_spec, o_spec = _mm_specs(b, t, d, d, 0)
    q = pl.pallas_call(
        _q_kernel, grid=grid, in_specs=[a_spec, wq_spec, nspec], out_specs=o_spec,
        out_shape=jax.ShapeDtypeStruct((b, t, d), BF16),
        compiler_params=_cparams(_MM_SEM), name="q_proj",
    )(hn, w_qkv, q_norm.reshape(1, HEAD_DIM))
    wk_spec = _mm_specs(b, t, d, d, d)[4]
    k, kb = pl.pallas_call(
        _k_kernel, grid=grid, in_specs=[a_spec, wk_spec, nspec], out_specs=[o_spec, o_spec],
        out_shape=[jax.ShapeDtypeStruct((b, t, d), F32), jax.ShapeDtypeStruct((b, t, d), BF16)],
        compiler_params=_cparams(_MM_SEM), name="k_proj",
    )(hn, w_qkv, k_norm.reshape(1, HEAD_DIM))
    wv_spec = _mm_specs(b, t, d, d, 2 * d)[4]
    if v_transposed:
        vb_spec = pl.BlockSpec((1, tn, tm), lambda bi, i, j: (bi, j, i))
        vb_shape = jax.ShapeDtypeStruct((b, d, t), BF16)
    else:
        vb_spec, vb_shape = o_spec, jax.ShapeDtypeStruct((b, t, d), BF16)
    v, vb = pl.pallas_call(
        functools.partial(_v_kernel, transposed=v_transposed), grid=grid,
        in_specs=[a_spec, wv_spec], out_specs=[o_spec, vb_spec],
        out_shape=[jax.ShapeDtypeStruct((b, t, d), F32), vb_shape],
        compiler_params=_cparams(_MM_SEM), name="v_proj",
    )(hn, w_qkv)
    return q, k, kb, v, vb


def _attn_kernel(qi_ref, ki_ref, last_ref, slope_ref, lam_ref, q_ref, k_ref, v_ref, sub_ref, o_ref,
                 m1_ref, l1_ref, a1_ref, m2_ref, l2_ref, a2_ref, *, tq, tk, past, out_scale):
    h = pl.program_id(1)
    step = pl.program_id(2)
    qi = qi_ref[step]
    ki = ki_ref[step]
    slope = slope_ref[h]
    q0 = past + qi * tq
    k0 = ki * tk

    @pl.when(ki == 0)
    def _():
        for m_ref, l_ref, a_ref in ((m1_ref, l1_ref, a1_ref), (m2_ref, l2_ref, a2_ref)):
            m_ref[...] = jnp.full(m_ref.shape, NEG_INF, F32)
            l_ref[...] = jnp.zeros(l_ref.shape, F32)
            a_ref[...] = jnp.zeros(a_ref.shape, F32)

    def update(bias, visible):
        v = v_ref[0]
        for idx, (m_ref, l_ref, a_ref) in enumerate(((m1_ref, l1_ref, a1_ref), (m2_ref, l2_ref, a2_ref))):
            qm = q_ref[0, :, idx * HEAD_DIM:(idx + 1) * HEAD_DIM]
            km = k_ref[0, :, idx * HEAD_DIM:(idx + 1) * HEAD_DIM]
            s = lax.dot_general(qm, km, (((1,), (1,)), ((), ())), preferred_element_type=F32) + bias
            if visible is not None:
                s = jnp.where(visible, s, NEG_INF)
            m_old = m_ref[...]
            m_new = jnp.maximum(m_old, jnp.max(s, axis=-1, keepdims=True))
            alpha = jnp.exp2(m_old - m_new)
            p = jnp.exp2(s - m_new)
            l_ref[...] = alpha * l_ref[...] + jnp.sum(p, axis=-1, keepdims=True)
            a_ref[...] = alpha * a_ref[...] + jnp.dot(p.astype(v.dtype), v, preferred_element_type=F32)
            m_ref[...] = m_new

    strictly_past = (k0 + tk - 1) <= q0

    @pl.when(strictly_past)
    def _():
        kpos = k0 + lax.broadcasted_iota(jnp.int32, (1, tk), 1)
        update(slope * (kpos - q0).astype(F32), None)

    @pl.when(jnp.logical_not(strictly_past))
    def _():
        qpos = q0 + lax.broadcasted_iota(jnp.int32, (tq, tk), 0)
        kpos = k0 + lax.broadcasted_iota(jnp.int32, (tq, tk), 1)
        bias = slope * ((qpos - q0) - jnp.abs(qpos - kpos)).astype(F32)
        update(bias, (kpos // CHUNK) <= (qpos // CHUNK))

    @pl.when(last_ref[step] == 1)
    def _():
        o = a1_ref[...] / l1_ref[...] - lam_ref[0] * (a2_ref[...] / l2_ref[...])
        o = o * lax.rsqrt(jnp.mean(o * o, axis=-1, keepdims=True) + NORM_EPS) * sub_ref[...]
        o_ref[0] = (o * out_scale).astype(o_ref.dtype)


def _attn_tables(tq_len, tk_len, past, tq, tk):
    qi_l, ki_l, last_l = [], [], []
    for qi in range(tq_len // tq):
        q_last_chunk = (past + qi * tq + tq - 1) // CHUNK
        ks = [ki for ki in range(tk_len // tk) if (ki * tk) // CHUNK <= q_last_chunk]
        for ki in ks:
            qi_l.append(qi)
            ki_l.append(ki)
            last_l.append(1 if ki == ks[-1] else 0)
    return (np.asarray(qi_l, np.int32), np.asarray(ki_l, np.int32), np.asarray(last_l, np.int32))


def _diff_attention(q, k, v, lam, subln, past, lam_init, tq, tk):
    b, tq_len, d = q.shape
    tk_len = k.shape[1]
    n_heads = d // (2 * HEAD_DIM)
    hw = 2 * HEAD_DIM
    qi_t, ki_t, last_t = _attn_tables(tq_len, tk_len, past, tq, tk)
    slopes = jnp.asarray(LOG2E * 2.0 ** (-8.0 * np.arange(1, n_heads + 1, dtype=np.float32) / n_heads), F32)
    grid_spec = pltpu.PrefetchScalarGridSpec(
        num_scalar_prefetch=5,
        grid=(b, n_heads, len(qi_t)),
        in_specs=[pl.BlockSpec((1, tq, hw), lambda bi, h, s, qi, ki, *_: (bi, qi[s], h)),
                  pl.BlockSpec((1, tk, hw), lambda bi, h, s, qi, ki, *_: (bi, ki[s], h)),
                  pl.BlockSpec((1, tk, hw), lambda bi, h, s, qi, ki, *_: (bi, ki[s], h)),
                  pl.BlockSpec((1, hw), lambda bi, h, s, *_: (0, 0))],
        out_specs=pl.BlockSpec((1, tq, hw), lambda bi, h, s, qi, ki, *_: (bi, qi[s], h)),
        scratch_shapes=[pltpu.VMEM((tq, 1), F32), pltpu.VMEM((tq, 1), F32), pltpu.VMEM((tq, hw), F32),
                        pltpu.VMEM((tq, 1), F32), pltpu.VMEM((tq, 1), F32), pltpu.VMEM((tq, hw), F32)],
    )
    return pl.pallas_call(
        functools.partial(_attn_kernel, tq=tq, tk=tk, past=past, out_scale=1.0 - lam_init),
        grid_spec=grid_spec,
        out_shape=jax.ShapeDtypeStruct((b, tq_len, d), BF16),
        compiler_params=_cparams(("arbitrary", "arbitrary", "arbitrary")), name="diff_attn",
    )(jnp.asarray(qi_t), jnp.asarray(ki_t), jnp.asarray(last_t), slopes, lam.reshape(1).astype(F32),
      q, k, v, subln.reshape(1, hw).astype(F32))


def _attn_t_kernel(qi_ref, ki_ref, last_ref, slope_ref, lam_ref, q_ref, k_ref, vt_ref, sub_ref, kp_ref, o_ref,
                   m_ref, l_ref, a_ref, *, tq, tk, out_scale):
    h = pl.program_id(1)
    step = pl.program_id(2)
    qi = qi_ref[step]
    ki = ki_ref[step]
    slope = slope_ref[h]
    q0 = qi * tq
    k0 = ki * tk
    lane_tiles = tq // 128

    @pl.when(ki == 0)
    def _():
        m_ref[...] = jnp.full(m_ref.shape, NEG_INF, F32)
        l_ref[...] = jnp.zeros(l_ref.shape, F32)
        a_ref[...] = jnp.zeros(a_ref.shape, F32)

    def update(bias_fn):
        vt = vt_ref[0]
        for idx in range(2):
            qm = q_ref[0, :, idx * HEAD_DIM:(idx + 1) * HEAD_DIM]
            km = k_ref[0, :, idx * HEAD_DIM:(idx + 1) * HEAD_DIM]
            st = bias_fn(lax.dot_general(km, qm, (((1,), (1,)), ((), ())), preferred_element_type=F32))
            m_old = m_ref[idx]
            m_new = jnp.maximum(m_old, jnp.max(st, axis=0, keepdims=True))
            alpha = jnp.exp2(m_old - m_new)
            pt = jnp.exp2(st - m_new)
            l_ref[idx] = alpha * l_ref[idx] + jnp.sum(pt, axis=0, keepdims=True)
            a_ref[idx] = alpha * a_ref[idx] + jnp.dot(vt, pt.astype(vt.dtype), preferred_element_type=F32)
            m_ref[idx] = m_new

    strictly_past = (k0 + tk - 1) <= q0

    @pl.when(strictly_past)
    def _():
        kb = slope * (kp_ref[...] - q0.astype(F32))
        update(lambda st: jnp.concatenate(
            [st[:, c * 128:(c + 1) * 128] + kb for c in range(lane_tiles)], axis=1))

    @pl.when(jnp.logical_not(strictly_past))
    def _():
        def bias_fn(st):
            kpos = k0 + lax.broadcasted_iota(jnp.int32, (tk, tq), 0)
            qpos = q0 + lax.broadcasted_iota(jnp.int32, (tk, tq), 1)
            bias = slope * ((qpos - q0) - jnp.abs(qpos - kpos)).astype(F32)
            return jnp.where((kpos // CHUNK) <= (qpos // CHUNK), st + bias, NEG_INF)
        update(bias_fn)

    @pl.when(last_ref[step] == 1)
    def _():
        o = a_ref[0] / l_ref[0] - lam_ref[0] * (a_ref[1] / l_ref[1])
        o = o * lax.rsqrt(jnp.mean(o * o, axis=0, keepdims=True) + NORM_EPS)
        o = jnp.concatenate([o[:, c * 128:(c + 1) * 128] * sub_ref[...] for c in range(lane_tiles)], axis=1)
        o_ref[0] = (o * out_scale).T.astype(o_ref.dtype)


def _diff_attention_t(q, k, vt, lam, subln, lam_init, tq, tk):
    b, t, d = q.shape
    n_heads = d // (2 * HEAD_DIM)
    hw = 2 * HEAD_DIM
    qi_t, ki_t, last_t = _attn_tables(t, t, 0, tq, tk)
    slopes = jnp.asarray(LOG2E * 2.0 ** (-8.0 * np.arange(1, n_heads + 1, dtype=np.float32) / n_heads), F32)
    key_pos = jnp.broadcast_to(jnp.arange(t, dtype=F32)[:, None], (t, 128))
    sub_rows = jnp.broadcast_to(subln.astype(F32).reshape(hw, 1), (hw, 128))
    grid_spec = pltpu.PrefetchScalarGridSpec(
        num_scalar_prefetch=5,
        grid=(b, n_heads, len(qi_t)),
        in_specs=[pl.BlockSpec((1, tq, hw), lambda bi, h, s, qi, ki, *_: (bi, qi[s], h)),
                  pl.BlockSpec((1, tk, hw), lambda bi, h, s, qi, ki, *_: (bi, ki[s], h)),
                  pl.BlockSpec((1, hw, tk), lambda bi, h, s, qi, ki, *_: (bi, h, ki[s])),
                  pl.BlockSpec((hw, 128), lambda bi, h, s, *_: (0, 0)),
                  pl.BlockSpec((tk, 128), lambda bi, h, s, qi, ki, *_: (ki[s], 0))],
        out_specs=pl.BlockSpec((1, tq, hw), lambda bi, h, s, qi, ki, *_: (bi, qi[s], h)),
        scratch_shapes=[pltpu.VMEM((2, 1, tq), F32), pltpu.VMEM((2, 1, tq), F32), pltpu.VMEM((2, hw, tq), F32)],
    )
    return pl.pallas_call(
        functools.partial(_attn_t_kernel, tq=tq, tk=tk, out_scale=1.0 - lam_init),
        grid_spec=grid_spec,
        out_shape=jax.ShapeDtypeStruct((b, t, d), BF16),
        compiler_params=_cparams(("arbitrary", "arbitrary", "arbitrary")), name="diff_attn_t",
    )(jnp.asarray(qi_t), jnp.asarray(ki_t), jnp.asarray(last_t), slopes, lam.reshape(1).astype(F32),
      q, k, vt, sub_rows, key_pos)


def _moe_kernel(be_ref, nu_ref, xs_ref, w1_ref, w3_ref, w2_ref, o_ref):
    blk = pl.program_id(0)

    @pl.when(blk < nu_ref[0])
    def _():
        x = xs_ref[...]
        h1 = jnp.dot(x, w1_ref[0, 0], preferred_element_type=F32)
        h3 = jnp.dot(x, w3_ref[0, 0], preferred_element_type=F32)
        hid = (h1 * jax.nn.sigmoid(h1)) * h3
        o_ref[...] = jnp.dot(hid.astype(BF16), w2_ref[0, 0], preferred_element_type=F32)

    @pl.when(blk >= nu_ref[0])
    def _():
        o_ref[...] = jnp.zeros(o_ref.shape, o_ref.dtype)


def _top2(x, axis):
    pos = lax.broadcasted_iota(jnp.int32, x.shape, axis)
    i0 = jnp.argmax(x, axis=axis).astype(jnp.int32)
    v0 = jnp.max(x, axis=axis)
    rest = jnp.where(pos == jnp.expand_dims(i0, axis), -jnp.inf, x)
    i1 = jnp.argmax(rest, axis=axis).astype(jnp.int32)
    v1 = jnp.max(rest, axis=axis)
    return (v0, v1), (i0, i1)


def _route(logits, router_b):
    n, e = logits.shape
    per = e // N_EXPERT_GROUPS
    scores = jax.nn.sigmoid(logits.T)
    sel_g = (scores + router_b.astype(F32)[:, None]).reshape(N_EXPERT_GROUPS, per, n)
    (v0, v1), _ = _top2(sel_g, 1)
    g_idx = jnp.argmax(v0 + v1, axis=0).astype(jnp.int32)
    in_group = jnp.take_along_axis(sel_g, g_idx[None, None, :], axis=0)[0]
    _, (l0, l1) = _top2(in_group, 0)
    e0, e1 = g_idx * per + l0, g_idx * per + l1
    w0 = jnp.take_along_axis(scores, e0[None, :], axis=0)[0]
    w1 = jnp.take_along_axis(scores, e1[None, :], axis=0)[0]
    total = w0 + w1
    return jnp.stack([e0, e1], axis=-1), jnp.stack([w0 / total, w1 / total], axis=-1)


def _cumsum_rows(mask):
    n, e = mask.shape
    rb = _tile(n, 128)
    m3 = mask.reshape(n // rb, rb, e).astype(BF16)
    tri = jnp.tril(jnp.ones((rb, rb), BF16))
    within = jnp.einsum('ij,bje->bie', tri, m3, preferred_element_type=F32).astype(jnp.int32)
    totals = within[:, -1, :]
    offs = jnp.cumsum(totals, axis=0) - totals
    return (within + offs[:, None, :]).reshape(n, e)


def _moe(hn, logits, router_b, w1, w3, w2, layer, bm):
    n, d = hn.shape
    e, f = w1.shape[1], w1.shape[3]
    expert_idx, gate = _route(logits, router_b)
    n_assign = n * TOP_K
    flat_e = expert_idx.reshape(n_assign)
    onehot = flat_e[:, None] == jnp.arange(e, dtype=jnp.int32)[None, :]
    csum = _cumsum_rows(onehot)
    counts = csum[-1]
    rank = jnp.take_along_axis(csum, flat_e[:, None], axis=1)[:, 0] - 1
    padded = (counts + bm - 1) // bm * bm
    pad_end = jnp.cumsum(padded)
    pad_start = pad_end - padded
    dest = pad_start[flat_e] + rank
    n_blocks = -(-n_assign // bm) + e
    slot_tok = jnp.zeros((n_blocks * bm,), jnp.int32).at[dest].set(
        jnp.arange(n_assign, dtype=jnp.int32) // TOP_K)
    xs = hn[slot_tok]
    block_start = jnp.arange(n_blocks, dtype=jnp.int32) * bm
    block_e = jnp.minimum(jnp.sum((pad_end[None, :] <= block_start[:, None]).astype(jnp.int32), axis=1), e - 1)
    n_used = (pad_end[-1] // bm).astype(jnp.int32).reshape(1)
    io_bytes = 2 * bm * d * (2 + 4) + 3 * bm * f * 4
    double = 2 * 3 * d * f * 2 + io_bytes <= MOE_VMEM_LIMIT
    once = pl.Buffered(2 if double else 1)
    grid_spec = pltpu.PrefetchScalarGridSpec(
        num_scalar_prefetch=2,
        grid=(n_blocks,),
        in_specs=[pl.BlockSpec((bm, d), lambda bi, be, nu: (bi, 0)),
                  pl.BlockSpec((1, 1, d, f), lambda bi, be, nu: (layer, be[bi], 0, 0), pipeline_mode=once),
                  pl.BlockSpec((1, 1, d, f), lambda bi, be, nu: (layer, be[bi], 0, 0), pipeline_mode=once),
                  pl.BlockSpec((1, 1, f, d), lambda bi, be, nu: (layer, be[bi], 0, 0), pipeline_mode=once)],
        out_specs=pl.BlockSpec((bm, d), lambda bi, be, nu: (bi, 0)),
    )
    ys = pl.pallas_call(
        _moe_kernel, grid_spec=grid_spec,
        out_shape=jax.ShapeDtypeStruct((n_blocks * bm, d), F32),
        compiler_params=pltpu.CompilerParams(dimension_semantics=("arbitrary",),
                                             vmem_limit_bytes=MOE_VMEM_LIMIT if double else VMEM_LIMIT),
        name="moe_experts",
    )(block_e, n_used, xs, w1, w3, w2)
    dest = dest.reshape(n, TOP_K)
    return gate[:, 0:1] * ys[dest[:, 0]] + gate[:, 1:2] * ys[dest[:, 1]]


def kernel(x_prompt, x_sample, c_prompt, c_sample, state_ssm_re, state_ssm_im, cache_k, cache_v, ada_w, ada_b, norm_mix, norm_ffn, ssm_a_re, ssm_a_im, ssm_log_dt, ssm_b_re, ssm_b_im, ssm_c_re, ssm_c_im, ssm_d, ssm_glu_w, ssm_glu_b, attn_w_qkv, attn_w_o, attn_q_norm, attn_k_norm, attn_lambda, attn_subln, router_w, router_b, moe_w1, moe_w3, moe_w2):
    depth, d = norm_mix.shape
    n_mixers = 2
    bp, bs = x_prompt.shape[0], x_sample.shape[0]
    assert bp == 1 and bs == SUBLANES, "the S5 scan maps one long sequence or eight sequences onto sublanes"
    n_heads = d // (2 * HEAD_DIM)
    g, p = ssm_a_re.shape[1:]
    past_len = cache_k.shape[2]

    r_all = -(-(bp + bs) // SUBLANES) * SUBLANES
    c_all = jnp.concatenate([c_prompt, c_sample, jnp.zeros((r_all - bp - bs, d), F32)])
    mod_all = _ada_mod(c_all, ada_w, ada_b).reshape(depth, r_all, 6, d)

    glu_w = _to_bf16(ssm_glu_w)
    w_qkv = _to_bf16(attn_w_qkv)
    w_o = _to_bf16(attn_w_o)
    w1, w3, w2 = _to_bf16(moe_w1), _to_bf16(moe_w3), _to_bf16(moe_w2)
    router_bf = router_w.astype(BF16)

    def trunk(x, row0, h0_re, h0_im, past_k, past_v, chained, bm):
        b, t, _ = x.shape
        new_re, new_im, new_k, new_v = [], [], [], []
        for i in range(depth):
            mod = mod_all[i, row0:row0 + b][:, :, None, :]
            shift1, scale1, gate1, shift2, scale2, gate2 = (mod[:, m] for m in range(6))
            j = i // n_mixers
            hn = _norm_mod(x, norm_mix[i], scale1, shift1, out_dtype=F32 if i % n_mixers == 0 else BF16)
            if i % n_mixers == 0:
                seg_len = b * (t // S5_L) // SUBLANES
                prm = _s5_params(ssm_a_re[j], ssm_a_im[j], ssm_log_dt[j], ssm_b_re[j], ssm_b_im[j],
                                 ssm_c_re[j], ssm_c_im[j], ssm_d[j], seg_len)
                if chained:
                    pad = jnp.zeros((SUBLANES - 1, g * p), F32)
                    i_re = jnp.concatenate([h0_re[j].reshape(1, g * p), pad])
                    i_im = jnp.concatenate([h0_im[j].reshape(1, g * p), pad])
                else:
                    i_re, i_im = h0_re[j].reshape(b, g * p), h0_im[j].reshape(b, g * p)
                gl, f_re, f_im = _s5_mixer(hn, i_re, i_im, prm, chained)
                if chained:
                    f_re, f_im = f_re[SUBLANES - 1:], f_im[SUBLANES - 1:]
                new_re.append(f_re.reshape(b, g, p))
                new_im.append(f_im.reshape(b, g, p))
                x = _glu_proj(gl, glu_w[j], ssm_glu_b[j], x, gate1)
            else:
                lam_init = 0.8 - 0.6 * math.exp(-0.3 * i)
                q, k, kb, v, vb = _qkv_proj(hn, w_qkv[j], attn_q_norm[j], attn_k_norm[j], past_k is None)
                new_k.append(k.reshape(b, t, n_heads, 2, HEAD_DIM))
                new_v.append(v.reshape(b, t, n_heads, 2 * HEAD_DIM))
                lp = attn_lambda[j].astype(F32)
                lam = jnp.exp(jnp.sum(lp[0] * lp[1])) - jnp.exp(jnp.sum(lp[2] * lp[3])) + lam_init
                if past_k is None:
                    o = _diff_attention_t(q, kb, vb, lam, attn_subln[j], lam_init,
                                          _tile(t, ATTN_TQ), _tile(t, ATTN_TK))
                else:
                    past = past_k.shape[2]
                    tk = _tile(past, ATTN_TK // 2)
                    fill = jnp.zeros((b, tk - t % tk if t % tk else 0, d), BF16)
                    k_all = jnp.concatenate([past_k[j].reshape(b, past, d).astype(BF16), kb, fill], axis=1)
                    v_all = jnp.concatenate([past_v[j].reshape(b, past, d).astype(BF16), vb, fill], axis=1)
                    o = _diff_attention(q, k_all, v_all, lam, attn_subln[j], past, lam_init, t, tk)
                x = _out_proj(o, w_o[j], x, gate1)
            hn, logits = _norm_mod(x, norm_ffn[i], scale2, shift2, router_bf)
            y = _moe(hn.reshape(b * t, d), logits.reshape(b * t, -1), router_b, w1, w3, w2, i, bm)
            x = x + gate2 * y.reshape(b, t, d)
        return x, jnp.stack(new_re), jnp.stack(new_im), jnp.stack(new_k), jnp.stack(new_v)

    zero_state = jnp.zeros((state_ssm_re.shape[0], bp, g, p), F32)
    y_p, re_p, im_p, k_p, v_p = trunk(x_prompt, 0, zero_state, zero_state, None, None, True, 256)
    y_s, re_s, im_s, k_s, v_s = trunk(x_sample, bp, state_ssm_re, state_ssm_im, cache_k, cache_v, False, 128)
    return (y_p, y_s, re_p, im_p, k_p, v_p, re_s, im_s, k_s, v_s)
```
